```python
import jax
import jax.numpy as jnp
from jax import lax
import numpy as np

D_MODEL = 1024
BATCH = 8
SEQ = 2048
DEPTH = 2
DEC_BATCH = 128
DEC_SEQ = 4
PAST_LEN = 8192
PAGE_SIZE = 128

N_A_LAYERS = DEPTH // 2
N_B_LAYERS = DEPTH - N_A_LAYERS
CONV_W = 31
CONV_HIST = CONV_W - 1
HEAD_DIM = 64
N_HEADS = D_MODEL // HEAD_DIM
N_KV_HEADS = 4
GQA_GROUP = N_HEADS // N_KV_HEADS
WINDOW = 128
N_GROUPS = 4
EXPERTS_PER_GROUP = 8
N_EXPERTS = N_GROUPS * EXPERTS_PER_GROUP
TOP_K = 2
D_EXPERT = D_MODEL // 2
RMS_EPS = 1e-5
LN_EPS = 1e-5
MASK_VALUE = -1e30

kernel_name = 'yoco_conformer_swa_sink_hmoe_step'


def _rms_norm(x, g):
    xf = x.astype(jnp.float32)
    y = xf * lax.rsqrt(jnp.mean(xf * xf, axis=-1, keepdims=True) + RMS_EPS)
    return (y * g.astype(jnp.float32)).astype(x.dtype)


def _layer_norm(x, g, b):
    xf = x.astype(jnp.float32)
    mu = jnp.mean(xf, axis=-1, keepdims=True)
    var = jnp.mean(jnp.square(xf - mu), axis=-1, keepdims=True)
    y = (xf - mu) * lax.rsqrt(var + LN_EPS)
    return (y * g.astype(jnp.float32) + b.astype(jnp.float32)).astype(x.dtype)


def _conformer_conv(h, hist, w_in, b_in, w_dw, b_dw, ln_g, ln_b, w_out, b_out):
    u = h @ w_in + b_in
    a, gate = jnp.split(u, 2, axis=-1)
    u = a * jax.nn.sigmoid(gate)
    full = jnp.concatenate([hist.astype(u.dtype), u], axis=1)
    y = lax.conv_general_dilated(
        full, w_dw[:, None, :].astype(full.dtype), window_strides=(1,), padding='VALID',
        dimension_numbers=('NWC', 'WIO', 'NWC'), feature_group_count=full.shape[-1]) + b_dw
    y = jax.nn.silu(_layer_norm(y, ln_g, ln_b))
    return y @ w_out + b_out, full[:, full.shape[1] - CONV_HIST:]


def _shared_kv(x, g, w_kv, b_kv):
    kv = _rms_norm(x, g) @ w_kv + b_kv
    k, v = jnp.split(kv, 2, axis=-1)
    shp = x.shape[:-1] + (N_KV_HEADS, HEAD_DIM)
    return k.reshape(shp), v.reshape(shp)


def _sink_attention(q, k, v, q_pos, k_pos, sinks):
    qg = q.reshape(q.shape[:-2] + (N_KV_HEADS, GQA_GROUP, HEAD_DIM))
    s = jnp.einsum('...qkgd,...skd->...kgqs', qg, k,
                   preferred_element_type=jnp.float32) * (HEAD_DIM ** -0.5)
    dist = q_pos[..., :, None] - k_pos[..., None, :]
    valid = (dist >= 0) & (dist < WINDOW) & (k_pos[..., None, :] >= 0)
    s = jnp.where(valid[..., None, None, :, :], s, MASK_VALUE)
    sink = sinks.astype(jnp.float32).reshape(N_KV_HEADS, GQA_GROUP, 1, 1)
    m = jnp.maximum(jnp.max(s, axis=-1, keepdims=True), sink)
    p = jnp.exp(s - m)
    p = p / (jnp.sum(p, axis=-1, keepdims=True) + jnp.exp(sink - m))
    o = jnp.einsum('...kgqs,...skd->...qkgd', p.astype(v.dtype), v)
    return o.reshape(o.shape[:-3] + (N_HEADS * HEAD_DIM,))


def _window_attn_prompt(h, k, v, w_q, b_q, sinks, w_o, b_o):
    bsz, seq, _ = h.shape
    nb = seq // WINDOW
    q = (h @ w_q + b_q).reshape(bsz, nb, WINDOW, N_HEADS, HEAD_DIM)

    def band(t):
        tb = jnp.pad(t, ((0, 0), (WINDOW, 0), (0, 0), (0, 0)))
        tb = tb.reshape(bsz, nb + 1, WINDOW, N_KV_HEADS, HEAD_DIM)
        return jnp.concatenate([tb[:, :-1], tb[:, 1:]], axis=2)

    pos = jnp.arange(-WINDOW, seq, dtype=jnp.int32).reshape(nb + 1, WINDOW)
    k_pos = jnp.concatenate([pos[:-1], pos[1:]], axis=-1)
    o = _sink_attention(q, band(k), band(v), pos[1:], k_pos, sinks)
    return o.reshape(bsz, seq, N_HEADS * HEAD_DIM) @ w_o + b_o


def _window_attn_sample(h, k_all, v_all, w_q, b_q, sinks, w_o, b_o):
    bsz, t, _ = h.shape
    tk = k_all.shape[1]
    q = (h @ w_q + b_q).reshape(bsz, t, N_HEADS, HEAD_DIM)
    q_pos = PAST_LEN + jnp.arange(t, dtype=jnp.int32)
    k_pos = (PAST_LEN + t - tk) + jnp.arange(tk, dtype=jnp.int32)
    o = _sink_attention(q, k_all, v_all, q_pos, k_pos, sinks)
    return o @ w_o + b_o


def _hier_moe(h, w_group, b_group, w_router, b_router, w1, w3, w2):
    hf = h.astype(jnp.float32)
    g_logits = hf @ w_group.astype(jnp.float32) + b_group.astype(jnp.float32)
    g_prob = jax.nn.softmax(g_logits, axis=-1)
    g_onehot = jax.nn.one_hot(jnp.argmax(g_logits, axis=-1), N_GROUPS, dtype=jnp.float32)
    g_w = jnp.sum(g_prob * g_onehot, axis=-1, keepdims=True)
    e_logits = jnp.einsum('...d,gde->...ge', hf, w_router.astype(jnp.float32)) + b_router.astype(jnp.float32)
    e_sel = jnp.einsum('...g,...ge->...e', g_onehot, e_logits)
    top_v, top_i = lax.top_k(e_sel, TOP_K)
    top_w = jax.nn.softmax(top_v, axis=-1) * g_w
    g_idx = jnp.argmax(g_onehot, axis=-1)
    expert_id = g_idx[..., None] * EXPERTS_PER_GROUP + top_i
    gate = jnp.sum(top_w[..., None] * jax.nn.one_hot(expert_id, N_EXPERTS, dtype=jnp.float32), axis=-2)
    out = jnp.zeros(h.shape, jnp.float32)
    for e in range(N_EXPERTS):
        act = jax.nn.silu(h @ w1[e]) * (h @ w3[e])
        out = out + gate[..., e:e + 1] * (act @ w2[e]).astype(jnp.float32)
    return out.astype(h.dtype)


def setup_inputs(seed: int = 0) -> dict:
    key = jax.random.key(seed)
    ks = iter(jax.random.split(key, 40))

    def nrm(shape, scale):
        return scale * jax.random.normal(next(ks), shape, jnp.float32)

    D = D_MODEL
    HW = N_HEADS * HEAD_DIM
    KVW = N_KV_HEADS * HEAD_DIM
    kv_buf = min(WINDOW, PAST_LEN)
    return {
        'x_prompt': nrm((BATCH, SEQ, D), 1.0),
        'x_sample': nrm((DEC_BATCH, DEC_SEQ, D), 1.0),
        'state_conv': nrm((N_A_LAYERS, DEC_BATCH, CONV_HIST, D), 0.5),
        'cache_k': nrm((DEC_BATCH, kv_buf, N_KV_HEADS, HEAD_DIM), 1.0),
        'cache_v': nrm((DEC_BATCH, kv_buf, N_KV_HEADS, HEAD_DIM), 1.0),
        'norm_mix': 1.0 + nrm((DEPTH, D), 0.02),
        'norm_ffn': 1.0 + nrm((DEPTH, D), 0.02),
        'conv_w_in': nrm((N_A_LAYERS, D, 2 * D), D ** -0.5),
        'conv_b_in': nrm((N_A_LAYERS, 2 * D), 0.02),
        'conv_w_dw': nrm((N_A_LAYERS, CONV_W, D), CONV_W ** -0.5),
        'conv_b_dw': nrm((N_A_LAYERS, D), 0.02),
        'conv_ln_g': 1.0 + nrm((N_A_LAYERS, D), 0.02),
        'conv_ln_b': nrm((N_A_LAYERS, D), 0.02),
        'conv_w_out': nrm((N_A_LAYERS, D, D), D ** -0.5),
        'conv_b_out': nrm((N_A_LAYERS, D), 0.02),
        'norm_kv': 1.0 + nrm((D,), 0.02),
        'w_kv': nrm((D, 2 * KVW), D ** -0.5),
        'b_kv': nrm((2 * KVW,), 0.02),
        'attn_w_q': nrm((N_B_LAYERS, D, HW), D ** -0.5),
        'attn_b_q': nrm((N_B_LAYERS, HW), 0.02),
        'attn_sinks': nrm((N_B_LAYERS, N_HEADS), 0.5),
        'attn_w_o': nrm((N_B_LAYERS, HW, D), HW ** -0.5),
        'attn_b_o': nrm((N_B_LAYERS, D), 0.02),
        'moe_w_group': nrm((DEPTH, D, N_GROUPS), D ** -0.5),
        'moe_b_group': nrm((DEPTH, N_GROUPS), 0.01),
        'moe_w_router': nrm((DEPTH, N_GROUPS, D, EXPERTS_PER_GROUP), D ** -0.5),
        'moe_b_router': nrm((DEPTH, N_GROUPS, EXPERTS_PER_GROUP), 0.01),
        'moe_w1': nrm((DEPTH, N_EXPERTS, D, D_EXPERT), D ** -0.5),
        'moe_w3': nrm((DEPTH, N_EXPERTS, D, D_EXPERT), D ** -0.5),
        'moe_w2': nrm((DEPTH, N_EXPERTS, D_EXPERT, D), D_EXPERT ** -0.5),
        'norm_final': 1.0 + nrm((D,), 0.02),
    }


def reference(x_prompt, x_sample, state_conv, cache_k, cache_v, norm_mix, norm_ffn,
              conv_w_in, conv_b_in, conv_w_dw, conv_b_dw, conv_ln_g, conv_ln_b, conv_w_out, conv_b_out,
              norm_kv, w_kv, b_kv, attn_w_q, attn_b_q, attn_sinks, attn_w_o, attn_b_o,
              moe_w_group, moe_b_group, moe_w_router, moe_b_router, moe_w1, moe_w3, moe_w2, norm_final):
    kv_buf = cache_k.shape[1]
    xp, xs = x_prompt, x_sample
    conv_p, conv_s = [], []
    for i in range(DEPTH):
        if i < N_A_LAYERS:
            cw = (conv_w_in[i], conv_b_in[i], conv_w_dw[i], conv_b_dw[i],
                  conv_ln_g[i], conv_ln_b[i], conv_w_out[i], conv_b_out[i])
            hist0 = jnp.zeros((xp.shape[0], CONV_HIST, xp.shape[-1]), xp.dtype)
            dp, hp = _conformer_conv(_rms_norm(xp, norm_mix[i]), hist0, *cw)
            ds, hs = _conformer_conv(_rms_norm(xs, norm_mix[i]), state_conv[i], *cw)
            xp = xp + dp
            xs = xs + ds
            conv_p.append(hp)
            conv_s.append(hs)
        else:
            if i == N_A_LAYERS:
                kp, vp = _shared_kv(xp, norm_kv, w_kv, b_kv)
                ks_new, vs_new = _shared_kv(xs, norm_kv, w_kv, b_kv)
                ks_all = jnp.concatenate([cache_k.astype(ks_new.dtype), ks_new], axis=1)
                vs_all = jnp.concatenate([cache_v.astype(vs_new.dtype), vs_new], axis=1)
            j = i - N_A_LAYERS
            aw = (attn_w_q[j], attn_b_q[j], attn_sinks[j], attn_w_o[j], attn_b_o[j])
            xp = xp + _window_attn_prompt(_rms_norm(xp, norm_mix[i]), kp, vp, *aw)
            xs = xs + _window_attn_sample(_rms_norm(xs, norm_mix[i]), ks_all, vs_all, *aw)
        mw = (moe_w_group[i], moe_b_group[i], moe_w_router[i], moe_b_router[i],
              moe_w1[i], moe_w3[i], moe_w2[i])
        xp = xp + _hier_moe(_rms_norm(xp, norm_ffn[i]), *mw)
        xs = xs + _hier_moe(_rms_norm(xs, norm_ffn[i]), *mw)
    y_prompt = _rms_norm(xp, norm_final)
    y_sample = _rms_norm(xs, norm_final)
    new_state_conv_prompt = jnp.stack(conv_p, axis=0)
    new_state_conv_sample = jnp.stack(conv_s, axis=0)
    new_k_prompt = kp[:, kp.shape[1] - kv_buf:]
    new_v_prompt = vp[:, vp.shape[1] - kv_buf:]
    new_k_sample = ks_all[:, ks_all.shape[1] - kv_buf:]
    new_v_sample = vs_all[:, vs_all.shape[1] - kv_buf:]
    return (y_prompt, y_sample, new_state_conv_prompt, new_state_conv_sample,
            new_k_prompt, new_v_prompt, new_k_sample, new_v_sample)
```

```python
import functools

import jax
import jax.numpy as jnp
from jax import lax
from jax.experimental import pallas as pl
from jax.experimental.pallas import tpu as pltpu

D = 1024
BATCH = 8
SEQ = 2048
DEC_BATCH = 128
DEC_SEQ = 4
CONV_W = 31
CONV_HIST = CONV_W - 1
HEAD_DIM = 64
N_HEADS = 16
N_KV = 4
GQA = 4
KVW = N_KV * HEAD_DIM
WINDOW = 128
N_GROUPS = 4
EPG = 8
N_EXPERTS = 32
D_EXPERT = 512
RMS_EPS = 1e-5
LN_EPS = 1e-5
MASK_VALUE = -1e30
PAST_LEN = 8192

TP = BATCH * SEQ
TS = DEC_BATCH * DEC_SEQ
T = TP + TS
N_ASSIGN = 2 * T

TILE_SEQ = 256
TILE_M = 256
N_TILES = (N_ASSIGN + N_EXPERTS * (TILE_M - 1)) // TILE_M + 1
P_ROWS = N_TILES * TILE_M
TILE_C = 256
TILE_SC = 1536
N_LOGIT_ROWS = 48
VMEM_LIMIT = 56 * 1024 * 1024

F32 = jnp.float32
BF16 = jnp.bfloat16


def _cparams(sem):
    return pltpu.CompilerParams(dimension_semantics=sem, vmem_limit_bytes=VMEM_LIMIT)


def _rms(x, g):
    return x * lax.rsqrt(jnp.mean(x * x, axis=-1, keepdims=True) + RMS_EPS) * g


def _dot(a, b):
    return jnp.dot(a, b, preferred_element_type=F32)


def _dot_nt(a, b):
    return lax.dot_general(a, b, (((1,), (1,)), ((), ())), preferred_element_type=F32)


def _moe_prologue(x, g_ffn, a_ref, rbias_ref, u_ref, cnt_ref):
    n = x.shape[0]
    h = _rms(x, g_ffn)
    h_hi = h.astype(BF16)
    h_lo = (h - h_hi.astype(F32)).astype(BF16)
    a = a_ref[...]
    l1 = _dot_nt(a, h_hi)
    l2 = _dot_nt(a[:N_LOGIT_ROWS], h_lo)
    logits = l1[:N_LOGIT_ROWS] + l1[N_LOGIT_ROWS:] + l2 + rbias_ref[:, 0:1]
    iota8 = lax.broadcasted_iota(jnp.int32, (8, n), 0).astype(F32)
    gl = logits[0:8]
    gmax = jnp.max(gl, axis=0, keepdims=True)
    g_idx = jnp.min(jnp.where(gl == gmax, iota8, 8.0), axis=0, keepdims=True)
    g_w = 1.0 / jnp.sum(jnp.exp(gl - gmax), axis=0, keepdims=True)
    es = jnp.where(g_idx == 0.0, logits[8:16],
                   jnp.where(g_idx == 1.0, logits[16:24],
                             jnp.where(g_idx == 2.0, logits[24:32], logits[32:40])))
    v1 = jnp.max(es, axis=0, keepdims=True)
    i1 = jnp.min(jnp.where(es == v1, iota8, 8.0), axis=0, keepdims=True)
    es2 = jnp.where(iota8 == i1, -jnp.inf, es)
    v2 = jnp.max(es2, axis=0, keepdims=True)
    i2 = jnp.min(jnp.where(es2 == v2, iota8, 8.0), axis=0, keepdims=True)
    e2x = jnp.exp(v2 - v1)
    w1 = g_w / (1.0 + e2x)
    w2 = g_w * e2x / (1.0 + e2x)
    ex1 = g_idx * EPG + i1
    ex2 = g_idx * EPG + i2
    iota32 = lax.broadcasted_iota(jnp.int32, (N_EXPERTS, n), 0).astype(F32)
    hit1 = iota32 == ex1
    hit2 = iota32 == ex2
    oh1 = jnp.where(hit1, 1.0, 0.0)
    oh2 = jnp.where(hit2, 1.0, 0.0)
    onehot = oh1 + oh2
    before = _dot(onehot.astype(BF16), u_ref[...]) + cnt_ref[:, 0:1]
    rank1 = jnp.sum(oh1 * before, axis=0, keepdims=True)
    rank2 = jnp.sum(oh2 * before, axis=0, keepdims=True)
    cnt_ref[...] = cnt_ref[...] + jnp.sum(onehot, axis=1, keepdims=True)
    rows = (ex1, ex2, w1, w2, rank1, rank2)
    meta = jnp.zeros((8, n), F32)
    for r, val in enumerate(rows):
        meta = jnp.where(iota8 == r, val, meta)
    return h, meta


def _ln_silu(y, g, b):
    mu = jnp.mean(y, axis=-1, keepdims=True)
    yc = y - mu
    var = jnp.mean(yc * yc, axis=-1, keepdims=True)
    z = yc * lax.rsqrt(var + LN_EPS) * g + b
    return z * jax.nn.sigmoid(z)


CONV_CH = 16


def _conv_prompt_body(x_ref, gmix_ref, win_ref, bin_ref, wdw_ref, bdw_ref, lng_ref, lnb_ref,
                      wout_ref, bout_ref, gffn_ref, a_ref, rbias_ref, u_ref,
                      x1_ref, h_ref, meta_ref, cnt_out_ref, state_ref,
                      ubuf, ush, ybuf, cnt_scr):
    b = pl.program_id(0)
    s = pl.program_id(1)
    ns = pl.num_programs(1)

    @pl.when((b == 0) & (s == 0))
    def _():
        cnt_scr[...] = jnp.zeros_like(cnt_scr)

    @pl.when(s == 0)
    def _():
        ubuf[0:32, :] = jnp.zeros((32, D), F32)

    x = x_ref[...]
    hn = _rms(x, gmix_ref[...])
    u2 = _dot(hn.astype(BF16), win_ref[...]) + bin_ref[...]
    u = u2[:, :D] * jax.nn.sigmoid(u2[:, D:])
    ubuf[32:32 + TILE_SEQ, :] = u

    for sh in range(1, 8):
        ush[sh - 1] = ubuf[sh:sh + TILE_SEQ + 24, :]

    def chunk(c, carry):
        r0 = pl.multiple_of(c * CONV_CH, CONV_CH)
        acc = jnp.zeros((CONV_CH, D), F32)
        for k in range(CONV_W):
            a8, sh = divmod(k + 2, 8)
            start = pl.multiple_of(r0 + 8 * a8, 8)
            if sh == 0:
                tap = ubuf[pl.ds(start, CONV_CH), :]
            else:
                tap = ush[sh - 1, pl.ds(start, CONV_CH), :]
            acc = acc + wdw_ref[k:k + 1, :] * tap
        ybuf[pl.ds(r0, CONV_CH), :] = acc
        return carry

    lax.fori_loop(0, TILE_SEQ // CONV_CH, chunk, 0)

    y = ybuf[...] + bdw_ref[...]
    act = _ln_silu(y, lng_ref[...], lnb_ref[...])
    x1 = x + _dot(act.astype(BF16), wout_ref[...]) + bout_ref[...]
    x1_ref[...] = x1

    @pl.when(s == ns - 1)
    def _():
        state_ref[0] = ubuf[TILE_SEQ + 2:TILE_SEQ + 32, :]

    ubuf[0:32, :] = ubuf[TILE_SEQ:TILE_SEQ + 32, :]

    h, meta = _moe_prologue(x1, gffn_ref[...], a_ref, rbias_ref, u_ref, cnt_scr)
    h_ref[...] = h
    meta_ref[...] = meta
    cnt_out_ref[...] = cnt_scr[...]


def _const_spec(shape):
    nd = len(shape)
    return pl.BlockSpec(shape, lambda *_: (0,) * nd)


def _conv_prompt(x, gmix, win, b_in, wdw, bdw, lng, lnb, wout, bout, gffn, a, rbias, u):
    nsq = SEQ // TILE_SEQ
    row = lambda b, s: (b * nsq + s, 0)
    consts = [gmix, win, b_in, wdw, bdw, lng, lnb, wout, bout, gffn, a, rbias, u]
    return pl.pallas_call(
        _conv_prompt_body,
        grid=(BATCH, nsq),
        in_specs=[pl.BlockSpec((TILE_SEQ, D), row)] + [_const_spec(c.shape) for c in consts],
        out_specs=[
            pl.BlockSpec((TILE_SEQ, D), row),
            pl.BlockSpec((TILE_SEQ, D), row),
            pl.BlockSpec((8, TILE_SEQ), lambda b, s: (0, b * nsq + s)),
            _const_spec((N_EXPERTS, 128)),
            pl.BlockSpec((1, CONV_HIST, D), lambda b, s: (b, 0, 0)),
        ],
        out_shape=[
            jax.ShapeDtypeStruct((T, D), F32),
            jax.ShapeDtypeStruct((T, D), F32),
            jax.ShapeDtypeStruct((8, T), F32),
            jax.ShapeDtypeStruct((N_EXPERTS, 128), F32),
            jax.ShapeDtypeStruct((BATCH, CONV_HIST, D), F32),
        ],
        scratch_shapes=[
            pltpu.VMEM((TILE_SEQ + 32, D), F32),
            pltpu.VMEM((7, TILE_SEQ + 24, D), F32),
            pltpu.VMEM((TILE_SEQ, D), F32),
            pltpu.VMEM((N_EXPERTS, 128), F32),
        ],
        compiler_params=_cparams(("arbitrary", "arbitrary")),
        name="conv_prompt",
    )(x, *consts)


def _glu_sample_body(x_ref, gmix_ref, win_ref, bin_ref, u_ref):
    hn = _rms(x_ref[...], gmix_ref[...])
    u2 = _dot(hn.astype(BF16), win_ref[...]) + bin_ref[...]
    u_ref[...] = u2[:, :D] * jax.nn.sigmoid(u2[:, D:])


def _glu_sample(x, gmix, win, b_in):
    args = [x, gmix, win, b_in]
    return pl.pallas_call(
        _glu_sample_body,
        grid=(1,),
        in_specs=[_const_spec(a.shape) for a in args],
        out_specs=_const_spec((TS, D)),
        out_shape=jax.ShapeDtypeStruct((TS, D), F32),
        compiler_params=_cparams(("arbitrary",)),
        name="glu_sample",
    )(*args)


SAMPLE_BB = 16


def _dwconv_sample_body(u_ref, st_ref, wh_ref, wu_ref, bdw_ref, y_ref, nst_ref):
    st = st_ref[...]
    u = u_ref[...]
    for t in range(DEC_SEQ):
        yt = (jnp.sum(st * wh_ref[t][None], axis=1, keepdims=True)
              + jnp.sum(u * wu_ref[t][None], axis=1, keepdims=True))
        y_ref[:, t:t + 1, :] = yt + bdw_ref[...][None]
    nst_ref[:, 0:CONV_HIST - DEC_SEQ, :] = st_ref[:, DEC_SEQ:CONV_HIST, :]
    nst_ref[:, CONV_HIST - DEC_SEQ:CONV_HIST, :] = u


def _dwconv_sample(u3, state, wh, wu, bdw):
    bb = SAMPLE_BB
    blk = lambda i: (i, 0, 0)
    return pl.pallas_call(
        _dwconv_sample_body,
        grid=(DEC_BATCH // bb,),
        in_specs=[
            pl.BlockSpec((bb, DEC_SEQ, D), blk),
            pl.BlockSpec((bb, CONV_HIST, D), blk),
            _const_spec(wh.shape), _const_spec(wu.shape), _const_spec(bdw.shape),
        ],
        out_specs=[pl.BlockSpec((bb, DEC_SEQ, D), blk), pl.BlockSpec((bb, CONV_HIST, D), blk)],
        out_shape=[
            jax.ShapeDtypeStruct((DEC_BATCH, DEC_SEQ, D), F32),
            jax.ShapeDtypeStruct((DEC_BATCH, CONV_HIST, D), F32),
        ],
        compiler_params=_cparams(("arbitrary",)),
        name="dwconv_sample",
    )(u3, state, wh, wu, bdw)


def _tail_sample_body(apply_ln, a_ref, x_ref, lng_ref, lnb_ref, w_ref, b_ref, gffn_ref,
                      ar_ref, rbias_ref, u_ref, cnt_in_ref, x1_in, h_in, meta_in,
                      x1_ref, h_ref, meta_ref, cnt_out_ref, cnt_scr):
    del x1_in, h_in, meta_in
    cnt_scr[...] = cnt_in_ref[...]
    a = a_ref[...]
    if apply_ln:
        a = _ln_silu(a, lng_ref[...], lnb_ref[...])
    x1 = x_ref[...] + _dot(a.astype(BF16), w_ref[...]) + b_ref[...]
    x1_ref[...] = x1
    h, meta = _moe_prologue(x1, gffn_ref[...], ar_ref, rbias_ref, u_ref, cnt_scr)
    h_ref[...] = h
    meta_ref[...] = meta
    cnt_out_ref[...] = cnt_scr[...]


def _tail_sample(apply_ln, a, x, lng, lnb, w, b, gffn, ar, rbias, u, cnt_in, x1_all, h_all, meta_all):
    small = [a, x, lng, lnb, w, b, gffn, ar, rbias, u, cnt_in]
    any_spec = pl.BlockSpec(memory_space=pl.ANY)
    tail_row = lambda i: (TP // TS, 0)
    return pl.pallas_call(
        functools.partial(_tail_sample_body, apply_ln),
        grid=(1,),
        in_specs=[_const_spec(v.shape) for v in small] + [any_spec, any_spec, any_spec],
        out_specs=[
            pl.BlockSpec((TS, D), tail_row),
            pl.BlockSpec((TS, D), tail_row),
            pl.BlockSpec((8, TS), lambda i: (0, TP // TS)),
            _const_spec((N_EXPERTS, 128)),
        ],
        out_shape=[
            jax.ShapeDtypeStruct((T, D), F32),
            jax.ShapeDtypeStruct((T, D), F32),
            jax.ShapeDtypeStruct((8, T), F32),
            jax.ShapeDtypeStruct((N_EXPERTS, 128), F32),
        ],
        scratch_shapes=[pltpu.VMEM((N_EXPERTS, 128), F32)],
        input_output_aliases={11: 0, 12: 1, 13: 2},
        compiler_params=_cparams(("arbitrary",)),
        name="tail_sample_ln" if apply_ln else "tail_sample",
    )(*small, x1_all, h_all, meta_all)


def _scatter_body(pos_ref, pad_ref, h_hbm, z_hbm, xs_hbm, sem):
    i = pl.program_id(0)

    @pl.when(i == 0)
    def _():
        def per_expert(e, total):
            zstart = pad_ref[e]
            nchunk = pad_ref[N_EXPERTS + e]

            def zissue(r, carry):
                dst = xs_hbm.at[pl.ds(pl.multiple_of(zstart + 8 * r, 8), 8), :]
                pltpu.make_async_copy(z_hbm, dst, sem).start()
                return carry

            lax.fori_loop(0, nchunk, zissue, 0)
            return total + nchunk

        total = lax.fori_loop(0, N_EXPERTS, per_expert, jnp.int32(0))

        @pl.when(total > 0)
        def _():
            nrows = pl.multiple_of(total * 8, 8)
            pltpu.make_async_copy(h_hbm.at[pl.ds(0, nrows), :],
                                  xs_hbm.at[pl.ds(0, nrows), :], sem).wait()

    @pl.when(i > 0)
    def _():
        base = (i - 1) * TILE_SC

        def issue(t, carry):
            src = h_hbm.at[pl.ds(base + t, 1), :]
            pltpu.make_async_copy(src, xs_hbm.at[pl.ds(pos_ref[base + t], 1), :], sem).start()
            pltpu.make_async_copy(src, xs_hbm.at[pl.ds(pos_ref[T + base + t], 1), :], sem).start()
            return carry

        lax.fori_loop(0, TILE_SC, issue, 0)
        pltpu.make_async_copy(h_hbm.at[pl.ds(0, 2 * TILE_SC), :],
                              xs_hbm.at[pl.ds(0, 2 * TILE_SC), :], sem).wait()


def _scatter_rows(pos, pad, h, zrow):
    any_spec = pl.BlockSpec(memory_space=pl.ANY)
    return pl.pallas_call(
        _scatter_body,
        grid_spec=pltpu.PrefetchScalarGridSpec(
            num_scalar_prefetch=2,
            grid=(1 + T // TILE_SC,),
            in_specs=[any_spec, any_spec],
            out_specs=any_spec,
            scratch_shapes=[pltpu.SemaphoreType.DMA(())],
        ),
        out_shape=jax.ShapeDtypeStruct((P_ROWS, D), F32),
        compiler_params=_cparams(("arbitrary",)),
        name="scatter_rows",
    )(pos, pad, h, zrow)


def _expert_body(te_ref, nu_ref, x_ref, w1_ref, w3_ref, w2_ref, y_ref, w1b, w3b, w2b):
    i = pl.program_id(0)
    live = i < nu_ref[0]
    prev = te_ref[jnp.maximum(i - 1, 0)]
    fresh = (i == 0) | (te_ref[i] != prev)

    @pl.when(live & fresh)
    def _():
        w1b[...] = w1_ref[0].astype(BF16)
        w3b[...] = w3_ref[0].astype(BF16)
        w2b[...] = w2_ref[0].astype(BF16)

    @pl.when(live)
    def _():
        xb = x_ref[...].astype(BF16)
        a = _dot(xb, w1b[...])
        g = _dot(xb, w3b[...])
        act = a * jax.nn.sigmoid(a) * g
        y_ref[...] = _dot(act.astype(BF16), w2b[...])


def _expert_ffn(tile_e, n_used, xs, w1, w3, w2):
    row = lambda i, te, nu: (jnp.minimum(i, nu[0] - 1), 0)
    wsel = lambda i, te, nu: (te[i], 0, 0)
    return pl.pallas_call(
        _expert_body,
        grid_spec=pltpu.PrefetchScalarGridSpec(
            num_scalar_prefetch=2,
            grid=(N_TILES,),
            in_specs=[
                pl.BlockSpec((TILE_M, D), row),
                pl.BlockSpec((1, D, D_EXPERT), wsel),
                pl.BlockSpec((1, D, D_EXPERT), wsel),
                pl.BlockSpec((1, D_EXPERT, D), wsel),
            ],
            out_specs=pl.BlockSpec((TILE_M, D), row),
            scratch_shapes=[
                pltpu.VMEM((D, D_EXPERT), BF16),
                pltpu.VMEM((D, D_EXPERT), BF16),
                pltpu.VMEM((D_EXPERT, D), BF16),
            ],
        ),
        out_shape=jax.ShapeDtypeStruct((P_ROWS, D), F32),
        compiler_params=_cparams(("arbitrary",)),
        name="expert_ffn",
    )(tile_e, n_used, xs, w1, w3, w2)


def _gather_issue(pos_ref, y_hbm, ybuf, sem, tile, slot):
    base = tile * TILE_C

    def issue(t, carry):
        for k in range(2):
            pltpu.make_async_copy(y_hbm.at[pl.ds(pos_ref[k * T + base + t], 1), :],
                                  ybuf.at[slot, k, pl.ds(t, 1), :], sem.at[slot]).start()
        return carry

    lax.fori_loop(0, TILE_C, issue, 0)


def _gather_wait(y_hbm, ybuf, sem, slot):
    for k in range(2):
        pltpu.make_async_copy(y_hbm.at[pl.ds(0, TILE_C), :], ybuf.at[slot, k], sem.at[slot]).wait()


def _combined(pos_ref, y_hbm, x_ref, wt_ref, ybuf, sem):
    i = pl.program_id(0)
    n = pl.num_programs(0)
    slot = i % 2

    @pl.when(i == 0)
    def _():
        _gather_issue(pos_ref, y_hbm, ybuf, sem, 0, 0)

    @pl.when(i + 1 < n)
    def _():
        _gather_issue(pos_ref, y_hbm, ybuf, sem, i + 1, 1 - slot)

    _gather_wait(y_hbm, ybuf, sem, slot)
    wt = wt_ref[...]
    return x_ref[...] + wt[:, 2:3] * ybuf[slot, 0] + wt[:, 3:4] * ybuf[slot, 1]


def _combine_kvq_body(pos_ref, y_hbm, x_ref, wt_ref, gkv_ref, wkv_ref, bkv_ref, gq_ref, wq_ref, bq_ref,
                      x2_ref, kv_ref, q_ref, ybuf, sem):
    x2 = _combined(pos_ref, y_hbm, x_ref, wt_ref, ybuf, sem)
    x2_ref[...] = x2
    kv_ref[...] = _dot(_rms(x2, gkv_ref[...]).astype(BF16), wkv_ref[...]) + bkv_ref[...]
    q_ref[...] = _dot(_rms(x2, gq_ref[...]).astype(BF16), wq_ref[...]) + bq_ref[...]


def _combine_kvq(pos, ys, x1, wt, gkv, wkv, bkv, gq, wq, bq):
    row = lambda i, p: (i, 0)
    consts = [gkv, wkv, bkv, gq, wq, bq]
    return pl.pallas_call(
        _combine_kvq_body,
        grid_spec=pltpu.PrefetchScalarGridSpec(
            num_scalar_prefetch=1,
            grid=(T // TILE_C,),
            in_specs=[pl.BlockSpec(memory_space=pl.ANY),
                      pl.BlockSpec((TILE_C, D), row),
                      pl.BlockSpec((TILE_C, 8), row)]
                     + [pl.BlockSpec(c.shape, lambda i, p, nd=c.ndim: (0,) * nd) for c in consts],
            out_specs=[pl.BlockSpec((TILE_C, D), row),
                       pl.BlockSpec((TILE_C, 2 * KVW), row),
                       pl.BlockSpec((TILE_C, D), row)],
            scratch_shapes=[pltpu.VMEM((2, 2, TILE_C, D), F32), pltpu.SemaphoreType.DMA((2,))],
        ),
        out_shape=[jax.ShapeDtypeStruct((T, D), F32),
                   jax.ShapeDtypeStruct((T, 2 * KVW), F32),
                   jax.ShapeDtypeStruct((T, D), F32)],
        compiler_params=_cparams(("arbitrary",)),
        name="combine_kvq",
    )(pos, ys, x1, wt, *consts)


def _combine_final_body(pos_ref, y_hbm, x_ref, wt_ref, gf_ref, yp_ref, ys_ref, ybuf, sem):
    i = pl.program_id(0)
    out = _rms(_combined(pos_ref, y_hbm, x_ref, wt_ref, ybuf, sem), gf_ref[...])

    @pl.when(i < TP // TILE_C)
    def _():
        yp_ref[...] = out

    @pl.when(i >= TP // TILE_C)
    def _():
        ys_ref[...] = out


def _combine_final(pos, ys, x3, wt, gf):
    row = lambda i, p: (i, 0)
    npt = TP // TILE_C
    return pl.pallas_call(
        _combine_final_body,
        grid_spec=pltpu.PrefetchScalarGridSpec(
            num_scalar_prefetch=1,
            grid=(T // TILE_C,),
            in_specs=[pl.BlockSpec(memory_space=pl.ANY),
                      pl.BlockSpec((TILE_C, D), row),
                      pl.BlockSpec((TILE_C, 8), row),
                      pl.BlockSpec((1, D), lambda i, p: (0, 0))],
            out_specs=[pl.BlockSpec((TILE_C, D), lambda i, p: (jnp.minimum(i, npt - 1), 0)),
                       pl.BlockSpec((TILE_C, D), lambda i, p: (jnp.maximum(i - npt, 0), 0))],
            scratch_shapes=[pltpu.VMEM((2, 2, TILE_C, D), F32), pltpu.SemaphoreType.DMA((2,))],
        ),
        out_shape=[jax.ShapeDtypeStruct((TP, D), F32), jax.ShapeDtypeStruct((TS, D), F32)],
        compiler_params=_cparams(("arbitrary",)),
        name="combine_final",
    )(pos, ys, x3, wt, gf)


def _lane_block_mask(rows_per_block, dtype):
    r = lax.broadcasted_iota(jnp.int32, (4 * rows_per_block, KVW), 0) // rows_per_block
    c = lax.broadcasted_iota(jnp.int32, (4 * rows_per_block, KVW), 1) // HEAD_DIM
    return (r == c).astype(dtype)


def _attn_core(q, kb, vb, valid, sink_col, rows):
    scale = HEAD_DIM ** -0.5
    bm = _lane_block_mask(rows, F32)
    outs = []
    for r in range(GQA):
        qr = q[:, r * KVW:(r + 1) * KVW]
        qm = (jnp.concatenate([qr] * N_KV, axis=0) * bm).astype(BF16)
        s = _dot_nt(qm, kb) * scale
        s = jnp.where(valid, s, MASK_VALUE)
        sink = sink_col[r * N_KV * rows:(r + 1) * N_KV * rows]
        m = jnp.maximum(jnp.max(s, axis=-1, keepdims=True), sink)
        p = jnp.exp(s - m)
        den = jnp.sum(p, axis=-1, keepdims=True) + jnp.exp(sink - m)
        o = _dot((p / den).astype(BF16), vb) * bm
        acc = o[0:rows]
        for g in range(1, N_KV):
            acc = acc + o[g * rows:(g + 1) * rows]
        outs.append(acc)
    return jnp.concatenate(outs, axis=1)


def _attn_prompt_body(q_ref, kvc_ref, kvp_ref, x_ref, sink_ref, wo_ref, bo_ref, gffn_ref,
                      a_ref, rbias_ref, u_ref, cnt_in_ref,
                      x3_ref, h_ref, meta_ref, cnt_out_ref, cnt_scr):
    b = pl.program_id(0)
    j = pl.program_id(1)

    @pl.when((b == 0) & (j == 0))
    def _():
        cnt_scr[...] = cnt_in_ref[...]

    kvc = kvc_ref[...]
    kvp = kvp_ref[...]
    kb = jnp.concatenate([kvp[:, :KVW], kvc[:, :KVW]], axis=0).astype(BF16)
    vb = jnp.concatenate([kvp[:, KVW:], kvc[:, KVW:]], axis=0).astype(BF16)
    qi = lax.broadcasted_iota(jnp.int32, (N_KV * WINDOW, 2 * WINDOW), 0) & (WINDOW - 1)
    kj = lax.broadcasted_iota(jnp.int32, (N_KV * WINDOW, 2 * WINDOW), 1)
    dist = qi + WINDOW - kj
    valid = (dist >= 0) & (dist < WINDOW) & ((kj >= WINDOW) | (j > 0))
    o = _attn_core(q_ref[...], kb, vb, valid, sink_ref[...], WINDOW)
    x3 = x_ref[...] + _dot(o.astype(BF16), wo_ref[...]) + bo_ref[...]
    x3_ref[...] = x3
    h, meta = _moe_prologue(x3, gffn_ref[...], a_ref, rbias_ref, u_ref, cnt_scr)
    h_ref[...] = h
    meta_ref[...] = meta
    cnt_out_ref[...] = cnt_scr[...]


def _attn_prompt(q, kv, x2, sink_col, wo, bo, gffn, a, rbias, u, cnt_in):
    nb = SEQ // WINDOW
    row = lambda b, j: (b * nb + j, 0)
    prev = lambda b, j: (b * nb + jnp.maximum(j - 1, 0), 0)
    consts = [sink_col, wo, bo, gffn, a, rbias, u, cnt_in]
    return pl.pallas_call(
        _attn_prompt_body,
        grid=(BATCH, nb),
        in_specs=[pl.BlockSpec((WINDOW, D), row),
                  pl.BlockSpec((WINDOW, 2 * KVW), row),
                  pl.BlockSpec((WINDOW, 2 * KVW), prev),
                  pl.BlockSpec((WINDOW, D), row)] + [_const_spec(c.shape) for c in consts],
        out_specs=[pl.BlockSpec((WINDOW, D), row),
                   pl.BlockSpec((WINDOW, D), row),
                   pl.BlockSpec((8, WINDOW), lambda b, j: (0, b * nb + j)),
                   _const_spec((N_EXPERTS, 128))],
        out_shape=[jax.ShapeDtypeStruct((T, D), F32),
                   jax.ShapeDtypeStruct((T, D), F32),
                   jax.ShapeDtypeStruct((8, T), F32),
                   jax.ShapeDtypeStruct((N_EXPERTS, 128), F32)],
        scratch_shapes=[pltpu.VMEM((N_EXPERTS, 128), F32)],
        compiler_params=_cparams(("arbitrary", "arbitrary")),
        name="attn_prompt",
    )(q, kv, kv, x2, *consts)


NK_S = WINDOW + 16


def _attn_sample_body(q_ref, kvn_ref, ck_ref, cv_ref, sink_ref, o_ref, nk_ref, nv_ref,
                      q8_scr, k_scr, v_scr):
    qi = lax.broadcasted_iota(jnp.int32, (N_KV * 8, NK_S), 0) & 7
    kj = lax.broadcasted_iota(jnp.int32, (N_KV * 8, NK_S), 1)
    in_cache = (kj < WINDOW) & (kj > qi)
    in_new = (kj >= WINDOW) & (kj - WINDOW <= qi) & (kj < WINDOW + DEC_SEQ)
    valid = (qi < DEC_SEQ) & (in_cache | in_new)
    q8_scr[...] = jnp.zeros_like(q8_scr)
    k_scr[...] = jnp.zeros_like(k_scr)
    v_scr[...] = jnp.zeros_like(v_scr)
    hist = WINDOW - DEC_SEQ

    def one(b, carry):
        k_scr[0:WINDOW, :] = ck_ref[b]
        v_scr[0:WINDOW, :] = cv_ref[b]
        k_scr[WINDOW:WINDOW + DEC_SEQ, :] = kvn_ref[b, :, 0:KVW]
        v_scr[WINDOW:WINDOW + DEC_SEQ, :] = kvn_ref[b, :, KVW:2 * KVW]
        q8_scr[0:DEC_SEQ, :] = q_ref[b]
        o8 = _attn_core(q8_scr[...], k_scr[...].astype(BF16), v_scr[...].astype(BF16),
                        valid, sink_ref[...], 8)
        o_ref[b] = o8[0:DEC_SEQ]
        nk_ref[b] = k_scr[DEC_SEQ:DEC_SEQ + WINDOW, :]
        nv_ref[b] = v_scr[DEC_SEQ:DEC_SEQ + WINDOW, :]
        return carry

    lax.fori_loop(0, SAMPLE_BB, one, 0)


def _attn_sample(q3, kvn3, ck, cv, sink_col8):
    bb = SAMPLE_BB
    blk = lambda i: (i, 0, 0)
    return pl.pallas_call(
        _attn_sample_body,
        grid=(DEC_BATCH // bb,),
        in_specs=[pl.BlockSpec((bb, DEC_SEQ, D), blk),
                  pl.BlockSpec((bb, DEC_SEQ, 2 * KVW), blk),
                  pl.BlockSpec((bb, WINDOW, KVW), blk),
                  pl.BlockSpec((bb, WINDOW, KVW), blk),
                  _const_spec(sink_col8.shape)],
        out_specs=[pl.BlockSpec((bb, DEC_SEQ, D), blk),
                   pl.BlockSpec((bb, WINDOW, KVW), blk),
                   pl.BlockSpec((bb, WINDOW, KVW), blk)],
        out_shape=[jax.ShapeDtypeStruct((DEC_BATCH, DEC_SEQ, D), F32),
                   jax.ShapeDtypeStruct((DEC_BATCH, WINDOW, KVW), F32),
                   jax.ShapeDtypeStruct((DEC_BATCH, WINDOW, KVW), F32)],
        scratch_shapes=[pltpu.VMEM((8, D), F32),
                        pltpu.VMEM((NK_S, KVW), F32),
                        pltpu.VMEM((NK_S, KVW), F32)],
        compiler_params=_cparams(("arbitrary",)),
        name="attn_sample",
    )(q3, kvn3, ck, cv, sink_col8)


def _router_weights(w_group, b_group, w_router, b_router):
    wt = jnp.zeros((N_LOGIT_ROWS, D), F32)
    wt = wt.at[0:N_GROUPS].set(w_group.T)
    wt = wt.at[8:8 + N_EXPERTS].set(jnp.transpose(w_router, (0, 2, 1)).reshape(N_EXPERTS, D))
    hi = wt.astype(BF16)
    lo = (wt - hi.astype(F32)).astype(BF16)
    a = jnp.concatenate([hi, lo], axis=0)
    bias = jnp.zeros((N_LOGIT_ROWS,), F32)
    bias = bias.at[0:N_GROUPS].set(b_group)
    bias = bias.at[N_GROUPS:8].set(MASK_VALUE)
    bias = bias.at[8:8 + N_EXPERTS].set(b_router.reshape(N_EXPERTS))
    return a, jnp.broadcast_to(bias[:, None], (N_LOGIT_ROWS, 128))


def _routing_tables(meta, cnt):
    counts = cnt[:, 0].astype(jnp.int32)
    padded = ((counts + TILE_M - 1) // TILE_M) * TILE_M
    ends = jnp.cumsum(padded)
    offs = ends - padded
    e0 = meta[0].astype(jnp.int32)
    e1 = meta[1].astype(jnp.int32)
    pos = jnp.concatenate([offs[e0] + meta[4].astype(jnp.int32),
                           offs[e1] + meta[5].astype(jnp.int32)])
    tile_start = jnp.arange(N_TILES, dtype=jnp.int32) * TILE_M
    n_used = (ends[-1] // TILE_M).astype(jnp.int32)
    tile_e = jnp.sum(tile_start[:, None] >= ends[None, :], axis=1).astype(jnp.int32)
    last_e = tile_e[jnp.maximum(n_used - 1, 0)]
    tile_e = jnp.where(tile_start < ends[-1], tile_e, last_e)
    zstart = (offs + counts) // 8 * 8
    pad = jnp.concatenate([zstart, (offs + padded - zstart) // 8]).astype(jnp.int32)
    wt = meta.T
    return pos, pad, tile_e, n_used.reshape(1), wt


def _moe_layer(h, meta, cnt, w1, w3, w2, zrow):
    pos, pad, tile_e, n_used, wt = _routing_tables(meta, cnt)
    xs = _scatter_rows(pos, pad, h, zrow)
    ys = _expert_ffn(tile_e, n_used, xs, w1, w3, w2)
    return pos, ys, wt


def _head_perm():
    idx = jnp.arange(D).reshape(N_KV, GQA, HEAD_DIM)
    return jnp.transpose(idx, (1, 0, 2)).reshape(D)


def kernel(x_prompt, x_sample, state_conv, cache_k, cache_v, norm_mix, norm_ffn, conv_w_in, conv_b_in, conv_w_dw, conv_b_dw, conv_ln_g, conv_ln_b, conv_w_out, conv_b_out, norm_kv, w_kv, b_kv, attn_w_q, attn_b_q, attn_sinks, attn_w_o, attn_b_o, moe_w_group, moe_b_group, moe_w_router, moe_b_router, moe_w1, moe_w3, moe_w2, norm_final):
    r2 = lambda v: v.reshape(1, -1)
    u_tri_s = jnp.triu(jnp.ones((TS, TS), BF16), 1)
    u_tri_c = u_tri_s[:TILE_SEQ, :TILE_SEQ]
    u_tri128 = u_tri_s[:WINDOW, :WINDOW]
    zrow = jnp.zeros((8, D), F32)
    ar0, rb0 = _router_weights(moe_w_group[0], moe_b_group[0], moe_w_router[0], moe_b_router[0])
    ar1, rb1 = _router_weights(moe_w_group[1], moe_b_group[1], moe_w_router[1], moe_b_router[1])

    win = conv_w_in[0].astype(BF16)
    wout = conv_w_out[0].astype(BF16)
    wdw = conv_w_dw[0]
    wdw32 = jnp.concatenate([wdw, jnp.zeros((1, D), F32)], axis=0)
    x1, h, meta, cnt, state_p = _conv_prompt(
        x_prompt.reshape(TP, D), r2(norm_mix[0]), win, r2(conv_b_in[0]), wdw32, r2(conv_b_dw[0]),
        r2(conv_ln_g[0]), r2(conv_ln_b[0]), wout, r2(conv_b_out[0]), r2(norm_ffn[0]), ar0, rb0, u_tri_c)

    xs2 = x_sample.reshape(TS, D)
    u_s = _glu_sample(xs2, r2(norm_mix[0]), win, r2(conv_b_in[0]))
    tt = jnp.arange(DEC_SEQ)[:, None]
    jh = jnp.arange(CONV_HIST)[None, :]
    wh = jnp.where((jh >= tt)[..., None], wdw[jnp.clip(jh - tt, 0, CONV_W - 1)], 0.0)
    ju = jnp.arange(DEC_SEQ)[None, :]
    wu = jnp.where((ju <= tt)[..., None], wdw[jnp.clip(CONV_HIST - tt + ju, 0, CONV_W - 1)], 0.0)
    y_s, state_s = _dwconv_sample(u_s.reshape(DEC_BATCH, DEC_SEQ, D), state_conv[0], wh, wu, r2(conv_b_dw[0]))
    x1, h, meta, cnt = _tail_sample(
        True, y_s.reshape(TS, D), xs2, r2(conv_ln_g[0]), r2(conv_ln_b[0]), wout, r2(conv_b_out[0]),
        r2(norm_ffn[0]), ar0, rb0, u_tri_s, cnt, x1, h, meta)

    pos, ys, wt = _moe_layer(h, meta, cnt, moe_w1[0], moe_w3[0], moe_w2[0], zrow)
    perm = _head_perm()
    wq = attn_w_q[0][:, perm].astype(BF16)
    bq = r2(attn_b_q[0][perm])
    wo = attn_w_o[0][perm, :].astype(BF16)
    x2, kv, q = _combine_kvq(pos, ys, x1, wt, r2(norm_kv), w_kv.astype(BF16), r2(b_kv),
                             r2(norm_mix[1]), wq, bq)

    sinks = attn_sinks[0].astype(F32).reshape(N_KV, GQA).T
    sink_col = jnp.repeat(sinks.reshape(-1), WINDOW)[:, None]
    sink_col8 = jnp.repeat(sinks.reshape(-1), 8)[:, None]
    zcnt = jnp.zeros((N_EXPERTS, 128), F32)
    x3, h, meta, cnt = _attn_prompt(q, kv, x2, sink_col, wo, r2(attn_b_o[0]), r2(norm_ffn[1]),
                                    ar1, rb1, u_tri128, zcnt)
    q_s = q[TP:].reshape(DEC_BATCH, DEC_SEQ, D)
    kv_s = kv[TP:].reshape(DEC_BATCH, DEC_SEQ, 2 * KVW)
    o_s, nk_s, nv_s = _attn_sample(q_s, kv_s, cache_k.reshape(DEC_BATCH, WINDOW, KVW),
                                   cache_v.reshape(DEC_BATCH, WINDOW, KVW), sink_col8)
    one = jnp.ones((1, D), F32)
    zero = jnp.zeros((1, D), F32)
    x3, h, meta, cnt = _tail_sample(
        False, o_s.reshape(TS, D), x2[TP:], one, zero, wo, r2(attn_b_o[0]),
        r2(norm_ffn[1]), ar1, rb1, u_tri_s, cnt, x3, h, meta)

    pos, ys, wt = _moe_layer(h, meta, cnt, moe_w1[1], moe_w3[1], moe_w2[1], zrow)
    y_p, y_s2 = _combine_final(pos, ys, x3, wt, r2(norm_final))

    kvp = kv[:TP].reshape(BATCH, SEQ, 2 * KVW)[:, SEQ - WINDOW:]
    new_k_p = kvp[..., :KVW].reshape(BATCH, WINDOW, N_KV, HEAD_DIM)
    new_v_p = kvp[..., KVW:].reshape(BATCH, WINDOW, N_KV, HEAD_DIM)
    return (y_p.reshape(BATCH, SEQ, D), y_s2.reshape(DEC_BATCH, DEC_SEQ, D),
            state_p[None], state_s[None], new_k_p, new_v_p,
            nk_s.reshape(DEC_BATCH, WINDOW, N_KV, HEAD_DIM), nv_s.reshape(DEC_BATCH, WINDOW, N_KV, HEAD_DIM))
```

```python
import jax
import jax.numpy as jnp
from jax import lax
from jax.experimental import pallas as pl
from jax.experimental.pallas import tpu as pltpu

D = 1024
BATCH = 8
SEQ = 2048
DEC_BATCH = 128
DEC_SEQ = 4
CONV_W = 31
CONV_HIST = CONV_W - 1
HEAD_DIM = 64
N_HEADS = 16
N_KV = 4
GQA = 4
KVW = N_KV * HEAD_DIM
WINDOW = 128
N_GROUPS = 4
EPG = 8
N_EXPERTS = 32
D_EXPERT = 512
RMS_EPS = 1e-5
LN_EPS = 1e-5
MASK_VALUE = -1e30

TP = BATCH * SEQ
TS = DEC_BATCH * DEC_SEQ
T = TP + TS
N_ASSIGN = 2 * T

TILE_SEQ = 256
TILE_M = 256
N_TILES = (N_ASSIGN + N_EXPERTS * (TILE_M - 1)) // TILE_M + 1
P_ROWS = N_TILES * TILE_M
TILE_C = 256
TILE_SC = 1536
N_LOGIT_ROWS = 48
SAMPLE_BB = 16
NKN = 16
ISSUE_UNROLL = 8
VMEM_LIMIT = 56 * 1024 * 1024

F32 = jnp.float32
BF16 = jnp.bfloat16


def _cparams(n_axes=1):
    return pltpu.CompilerParams(dimension_semantics=("arbitrary",) * n_axes,
                                vmem_limit_bytes=VMEM_LIMIT)


def _const_spec(shape):
    nd = len(shape)
    return pl.BlockSpec(shape, lambda *_: (0,) * nd)


def _rms(x, g):
    return x * lax.rsqrt(jnp.mean(x * x, axis=-1, keepdims=True) + RMS_EPS) * g


def _dot(a, b):
    return jnp.dot(a, b, preferred_element_type=F32)


def _dot_nt(a, b):
    return lax.dot_general(a, b, (((1,), (1,)), ((), ())), preferred_element_type=F32)


def _moe_prologue(x, g_ffn, a_ref, rbias_ref, u_ref, cnt_ref):
    n = x.shape[0]
    h = _rms(x, g_ffn)
    h_hi = h.astype(BF16)
    h_lo = (h - h_hi.astype(F32)).astype(BF16)
    a = a_ref[...]
    l1 = _dot_nt(a, h_hi)
    l2 = _dot_nt(a[:N_LOGIT_ROWS], h_lo)
    logits = l1[:N_LOGIT_ROWS] + l1[N_LOGIT_ROWS:] + l2 + rbias_ref[:, 0:1]
    iota8 = lax.broadcasted_iota(jnp.int32, (8, n), 0).astype(F32)
    gl = logits[0:8]
    gmax = jnp.max(gl, axis=0, keepdims=True)
    g_idx = jnp.min(jnp.where(gl == gmax, iota8, 8.0), axis=0, keepdims=True)
    g_w = 1.0 / jnp.sum(jnp.exp(gl - gmax), axis=0, keepdims=True)
    es = jnp.where(g_idx == 0.0, logits[8:16],
                   jnp.where(g_idx == 1.0, logits[16:24],
                             jnp.where(g_idx == 2.0, logits[24:32], logits[32:40])))
    v1 = jnp.max(es, axis=0, keepdims=True)
    i1 = jnp.min(jnp.where(es == v1, iota8, 8.0), axis=0, keepdims=True)
    es2 = jnp.where(iota8 == i1, -jnp.inf, es)
    v2 = jnp.max(es2, axis=0, keepdims=True)
    i2 = jnp.min(jnp.where(es2 == v2, iota8, 8.0), axis=0, keepdims=True)
    e2x = jnp.exp(v2 - v1)
    w1 = g_w / (1.0 + e2x)
    w2 = g_w * e2x / (1.0 + e2x)
    ex1 = g_idx * EPG + i1
    ex2 = g_idx * EPG + i2
    iota32 = lax.broadcasted_iota(jnp.int32, (N_EXPERTS, n), 0).astype(F32)
    oh1 = jnp.where(iota32 == ex1, 1.0, 0.0)
    oh2 = jnp.where(iota32 == ex2, 1.0, 0.0)
    onehot = oh1 + oh2
    before = _dot(onehot.astype(BF16), u_ref[...]) + cnt_ref[:, 0:1]
    rank1 = jnp.sum(oh1 * before, axis=0, keepdims=True)
    rank2 = jnp.sum(oh2 * before, axis=0, keepdims=True)
    cnt_ref[...] = cnt_ref[...] + jnp.sum(onehot, axis=1, keepdims=True)
    rows = (ex1, ex2, w1, w2, rank1, rank2)
    meta = jnp.zeros((8, n), F32)
    for r, val in enumerate(rows):
        meta = jnp.where(iota8 == r, val, meta)
    return h, meta


def _ln_silu(y, g, b):
    mu = jnp.mean(y, axis=-1, keepdims=True)
    yc = y - mu
    var = jnp.mean(yc * yc, axis=-1, keepdims=True)
    z = yc * lax.rsqrt(var + LN_EPS) * g + b
    return z * jax.nn.sigmoid(z)


CONV_CH = 32
NP_CONV = TP // TILE_SEQ
NSQ = SEQ // TILE_SEQ


def _conv_mixer_body(xp_ref, xs_ref, ys_ref, gmix_ref, win_ref, bin_ref, wdw_ref, bdw_ref,
                     lng_ref, lnb_ref, wout_ref, bout_ref, gffn_ref, a_ref, rbias_ref, u_ref,
                     x1_ref, h_ref, meta_ref, cnt_out_ref, state_ref,
                     ubuf, ush, ybuf, cnt_scr):
    i = pl.program_id(0)
    is_prompt = i < NP_CONV
    s = i % NSQ

    @pl.when(i == 0)
    def _():
        cnt_scr[...] = jnp.zeros_like(cnt_scr)

    @pl.when(is_prompt)
    def _():
        @pl.when(s == 0)
        def _():
            ubuf[0:32, :] = jnp.zeros((32, D), F32)

        hn = _rms(xp_ref[...], gmix_ref[...])
        u2 = _dot(hn.astype(BF16), win_ref[...]) + bin_ref[...]
        ubuf[32:32 + TILE_SEQ, :] = u2[:, :D] * jax.nn.sigmoid(u2[:, D:])

        for sh in range(1, 8):
            ush[sh - 1] = ubuf[sh:sh + TILE_SEQ + 24, :]

        ngrp = CONV_CH // 8

        def chunk(c, carry):
            r0 = pl.multiple_of(c * CONV_CH, CONV_CH)
            for lt in range(D // 128):
                lanes = slice(lt * 128, (lt + 1) * 128)
                acc = [None] * ngrp
                for sh in range(8):
                    a8s = [a8 for a8 in range(5) if 0 <= 8 * a8 + sh - 2 < CONV_W]
                    win = {}
                    for gi in range(a8s[0], a8s[-1] + ngrp):
                        start = pl.multiple_of(r0 + 8 * gi, 8)
                        if sh == 0:
                            win[gi] = ubuf[pl.ds(start, 8), lanes]
                        else:
                            win[gi] = ush[sh - 1, pl.ds(start, 8), lanes]
                    for a8 in a8s:
                        w = wdw_ref[8 * a8 + sh - 2, :, lanes]
                        for gq in range(ngrp):
                            term = w * win[a8 + gq]
                            acc[gq] = term if acc[gq] is None else acc[gq] + term
                for gq in range(ngrp):
                    ybuf[pl.ds(pl.multiple_of(r0 + 8 * gq, 8), 8), lanes] = acc[gq]
            return carry

        lax.fori_loop(0, TILE_SEQ // CONV_CH, chunk, 0)

        @pl.when(s == NSQ - 1)
        def _():
            state_ref[0] = ubuf[TILE_SEQ + 2:TILE_SEQ + 32, :]

        ubuf[0:32, :] = ubuf[TILE_SEQ:TILE_SEQ + 32, :]

    @pl.when(jnp.logical_not(is_prompt))
    def _():
        ybuf[...] = ys_ref[...]

    x = jnp.where(is_prompt, xp_ref[...], xs_ref[...])
    act = _ln_silu(ybuf[...] + bdw_ref[...], lng_ref[...], lnb_ref[...])
    x1 = x + _dot(act.astype(BF16), wout_ref[...]) + bout_ref[...]
    x1_ref[...] = x1
    h, meta = _moe_prologue(x1, gffn_ref[...], a_ref, rbias_ref, u_ref, cnt_scr)
    h_ref[...] = h
    meta_ref[...] = meta
    cnt_out_ref[...] = cnt_scr[...]


def _conv_mixer(xp, xs, ys, gmix, win, b_in, wdw, bdw, lng, lnb, wout, bout, gffn, a, rbias, u):
    consts = [gmix, win, b_in, wdw, bdw, lng, lnb, wout, bout, gffn, a, rbias, u]
    prow = lambda i: (jnp.minimum(i, NP_CONV - 1), 0)
    srow = lambda i: (jnp.maximum(i - NP_CONV, 0), 0)
    row = lambda i: (i, 0)
    return pl.pallas_call(
        _conv_mixer_body,
        grid=(T // TILE_SEQ,),
        in_specs=[pl.BlockSpec((TILE_SEQ, D), prow),
                  pl.BlockSpec((TILE_SEQ, D), srow),
                  pl.BlockSpec((TILE_SEQ, D), srow)] + [_const_spec(c.shape) for c in consts],
        out_specs=[
            pl.BlockSpec((TILE_SEQ, D), row),
            pl.BlockSpec((TILE_SEQ, D), row),
            pl.BlockSpec((8, TILE_SEQ), lambda i: (0, i)),
            _const_spec((N_EXPERTS, 128)),
            pl.BlockSpec((1, CONV_HIST, D), lambda i: (jnp.minimum(i // NSQ, BATCH - 1), 0, 0)),
        ],
        out_shape=[
            jax.ShapeDtypeStruct((T, D), F32),
            jax.ShapeDtypeStruct((T, D), F32),
            jax.ShapeDtypeStruct((8, T), F32),
            jax.ShapeDtypeStruct((N_EXPERTS, 128), F32),
            jax.ShapeDtypeStruct((BATCH, CONV_HIST, D), F32),
        ],
        scratch_shapes=[
            pltpu.VMEM((TILE_SEQ + 32, D), F32),
            pltpu.VMEM((7, TILE_SEQ + 24, D), F32),
            pltpu.VMEM((TILE_SEQ, D), F32),
            pltpu.VMEM((N_EXPERTS, 128), F32),
        ],
        compiler_params=_cparams(),
        name="conv_mixer",
    )(xp, xs, ys, *consts)


def _glu_sample_body(x_ref, gmix_ref, win_ref, bin_ref, u_ref):
    hn = _rms(x_ref[...], gmix_ref[...])
    u2 = _dot(hn.astype(BF16), win_ref[...]) + bin_ref[...]
    u_ref[...] = u2[:, :D] * jax.nn.sigmoid(u2[:, D:])


def _glu_sample(x, gmix, win, b_in):
    args = [x, gmix, win, b_in]
    return pl.pallas_call(
        _glu_sample_body,
        grid=(1,),
        in_specs=[_const_spec(a.shape) for a in args],
        out_specs=_const_spec((TS, D)),
        out_shape=jax.ShapeDtypeStruct((TS, D), F32),
        compiler_params=_cparams(),
        name="glu_sample",
    )(*args)


def _dwconv_sample_body(u_ref, st_ref, wh_ref, wu_ref, y_ref, nst_ref):
    st = st_ref[0]
    u = u_ref[...]
    for t in range(DEC_SEQ):
        y_ref[:, t:t + 1, :] = (jnp.sum(st * wh_ref[t][None], axis=1, keepdims=True)
                                + jnp.sum(u * wu_ref[t][None], axis=1, keepdims=True))
    nst_ref[0, :, 0:CONV_HIST - DEC_SEQ, :] = st_ref[0, :, DEC_SEQ:CONV_HIST, :]
    nst_ref[0, :, CONV_HIST - DEC_SEQ:CONV_HIST, :] = u


def _dwconv_sample(u3, state, wh, wu):
    bb = SAMPLE_BB
    blk = lambda i: (i, 0, 0)
    sblk = lambda i: (0, i, 0, 0)
    return pl.pallas_call(
        _dwconv_sample_body,
        grid=(DEC_BATCH // bb,),
        in_specs=[
            pl.BlockSpec((bb, DEC_SEQ, D), blk),
            pl.BlockSpec((1, bb, CONV_HIST, D), sblk),
            _const_spec(wh.shape), _const_spec(wu.shape),
        ],
        out_specs=[pl.BlockSpec((bb, DEC_SEQ, D), blk), pl.BlockSpec((1, bb, CONV_HIST, D), sblk)],
        out_shape=[
            jax.ShapeDtypeStruct((DEC_BATCH, DEC_SEQ, D), F32),
            jax.ShapeDtypeStruct((1, DEC_BATCH, CONV_HIST, D), F32),
        ],
        compiler_params=_cparams(),
        name="dwconv_sample",
    )(u3, state, wh, wu)


def _positions_body(offs_ref, meta_ref, pos_ref):
    ex = meta_ref[0:2, :]
    acc = meta_ref[4:6, :]
    for e in range(N_EXPERTS):
        acc = acc + jnp.where(ex == float(e), offs_ref[e].astype(F32), 0.0)
    pos_ref[...] = acc.astype(jnp.int32)


def _positions(offs, meta):
    return pl.pallas_call(
        _positions_body,
        grid_spec=pltpu.PrefetchScalarGridSpec(
            num_scalar_prefetch=1,
            grid=(1,),
            in_specs=[pl.BlockSpec((8, T), lambda i, o: (0, 0))],
            out_specs=pl.BlockSpec((2, T), lambda i, o: (0, 0)),
        ),
        out_shape=jax.ShapeDtypeStruct((2, T), jnp.int32),
        compiler_params=_cparams(),
        name="positions",
    )(offs, meta)


def _scatter_body(pos_ref, pad_ref, h_ref, xs_hbm, zbuf, sem):
    i = pl.program_id(0)

    @pl.when(i == 0)
    def _():
        zbuf[...] = jnp.zeros_like(zbuf)

        def per_region(e, total):
            zstart = pad_ref[e]
            nchunk = pad_ref[N_EXPERTS + 1 + e]

            def zissue(r, carry):
                dst = xs_hbm.at[pl.ds(pl.multiple_of(zstart + 8 * r, 8), 8), :]
                pltpu.make_async_copy(zbuf, dst, sem).start()
                return carry

            lax.fori_loop(0, nchunk, zissue, 0)
            return total + nchunk

        total = lax.fori_loop(0, N_EXPERTS + 1, per_region, jnp.int32(0))

        @pl.when(total > 0)
        def _():
            nrows = pl.multiple_of(total * 8, 8)
            pltpu.make_async_copy(xs_hbm.at[pl.ds(0, nrows), :],
                                  xs_hbm.at[pl.ds(0, nrows), :], sem).wait()

    @pl.when(i > 0)
    def _():
        base = (i - 1) * TILE_SC

        def issue(c, carry):
            for j in range(ISSUE_UNROLL):
                t = c * ISSUE_UNROLL + j
                src = h_ref.at[pl.ds(t, 1), :]
                for k in range(2):
                    dst = xs_hbm.at[pl.ds(pos_ref[k * T + base + t], 1), :]
                    pltpu.make_async_copy(src, dst, sem).start()
            return carry

        lax.fori_loop(0, TILE_SC // ISSUE_UNROLL, issue, 0)
        for _ in range(2):
            pltpu.make_async_copy(h_ref, xs_hbm.at[pl.ds(0, TILE_SC), :], sem).wait()


def _scatter_rows(pos, pad, h):
    return pl.pallas_call(
        _scatter_body,
        grid_spec=pltpu.PrefetchScalarGridSpec(
            num_scalar_prefetch=2,
            grid=(1 + T // TILE_SC,),
            in_specs=[pl.BlockSpec((TILE_SC, D), lambda i, p, q: (jnp.maximum(i - 1, 0), 0))],
            out_specs=pl.BlockSpec(memory_space=pl.ANY),
            scratch_shapes=[pltpu.VMEM((8, D), F32), pltpu.SemaphoreType.DMA(())],
        ),
        out_shape=jax.ShapeDtypeStruct((P_ROWS, D), F32),
        compiler_params=_cparams(),
        name="scatter_rows",
    )(pos, pad, h)


def _expert_body(te_ref, nu_ref, x_ref, w1_ref, w3_ref, w2_ref, y_ref, w1b, w3b, w2b):
    i = pl.program_id(0)
    live = i < nu_ref[0]
    prev = te_ref[jnp.maximum(i - 1, 0)]
    fresh = (i == 0) | (te_ref[i] != prev)

    @pl.when(live & fresh)
    def _():
        w1b[...] = w1_ref[0, 0].astype(BF16)
        w3b[...] = w3_ref[0, 0].astype(BF16)
        w2b[...] = w2_ref[0, 0].astype(BF16)

    @pl.when(live)
    def _():
        xb = x_ref[...].astype(BF16)
        a = _dot(xb, w1b[...])
        g = _dot(xb, w3b[...])
        act = a * jax.nn.sigmoid(a) * g
        y_ref[...] = _dot(act.astype(BF16), w2b[...])

    @pl.when(jnp.logical_not(live))
    def _():
        y_ref[...] = jnp.zeros_like(y_ref)


def _expert_ffn(layer, tile_e, n_used, xs, w1, w3, w2):
    xrow = lambda i, te, nu: (jnp.minimum(i, nu[0] - 1), 0)
    wsel = lambda i, te, nu: (layer, te[i], 0, 0)
    return pl.pallas_call(
        _expert_body,
        grid_spec=pltpu.PrefetchScalarGridSpec(
            num_scalar_prefetch=2,
            grid=(N_TILES,),
            in_specs=[
                pl.BlockSpec((TILE_M, D), xrow),
                pl.BlockSpec((1, 1, D, D_EXPERT), wsel),
                pl.BlockSpec((1, 1, D, D_EXPERT), wsel),
                pl.BlockSpec((1, 1, D_EXPERT, D), wsel),
            ],
            out_specs=pl.BlockSpec((TILE_M, D), lambda i, te, nu: (i, 0)),
            scratch_shapes=[
                pltpu.VMEM((D, D_EXPERT), BF16),
                pltpu.VMEM((D, D_EXPERT), BF16),
                pltpu.VMEM((D_EXPERT, D), BF16),
            ],
        ),
        out_shape=jax.ShapeDtypeStruct((P_ROWS, D), F32),
        compiler_params=_cparams(),
        name="expert_ffn",
    )(tile_e, n_used, xs, w1, w3, w2)


def _gather_issue(pos_ref, y_hbm, ybuf, sem, tile, slot):
    base = tile * TILE_C

    def issue(c, carry):
        for j in range(ISSUE_UNROLL):
            t = c * ISSUE_UNROLL + j
            for k in range(2):
                pltpu.make_async_copy(y_hbm.at[pl.ds(pos_ref[k * T + base + t], 1), :],
                                      ybuf.at[slot, k, pl.ds(t, 1), :], sem.at[slot]).start()
        return carry

    lax.fori_loop(0, TILE_C // ISSUE_UNROLL, issue, 0)


def _gather_wait(y_hbm, ybuf, sem, slot):
    for k in range(2):
        pltpu.make_async_copy(y_hbm.at[pl.ds(0, TILE_C), :], ybuf.at[slot, k], sem.at[slot]).wait()


def _combined(pos_ref, y_hbm, x_ref, meta_ref, ybuf, sem):
    i = pl.program_id(0)
    n = pl.num_programs(0)
    slot = i % 2

    @pl.when(i == 0)
    def _():
        _gather_issue(pos_ref, y_hbm, ybuf, sem, 0, 0)

    @pl.when(i + 1 < n)
    def _():
        _gather_issue(pos_ref, y_hbm, ybuf, sem, i + 1, 1 - slot)

    _gather_wait(y_hbm, ybuf, sem, slot)
    mt = jnp.concatenate([meta_ref[...], jnp.zeros((120, TILE_C), F32)], axis=0).T
    return x_ref[...] + mt[:, 2:3] * ybuf[slot, 0] + mt[:, 3:4] * ybuf[slot, 1]


def _combine_kvq_body(pos_ref, y_hbm, x_ref, meta_ref, gkv_ref, wkv_ref, bkv_ref, gq_ref, wq_ref, bq_ref,
                      x2_ref, kv_ref, q_ref, ybuf, sem):
    x2 = _combined(pos_ref, y_hbm, x_ref, meta_ref, ybuf, sem)
    x2_ref[...] = x2
    kv_ref[...] = _dot(_rms(x2, gkv_ref[...]).astype(BF16), wkv_ref[...]) + bkv_ref[...]
    q_ref[...] = _dot(_rms(x2, gq_ref[...]).astype(BF16), wq_ref[...]) + bq_ref[...]


def _combine_kvq(pos, ys, x1, meta, gkv, wkv, bkv, gq, wq, bq):
    row = lambda i, p: (i, 0)
    consts = [gkv, wkv, bkv, gq, wq, bq]
    return pl.pallas_call(
        _combine_kvq_body,
        grid_spec=pltpu.PrefetchScalarGridSpec(
            num_scalar_prefetch=1,
            grid=(T // TILE_C,),
            in_specs=[pl.BlockSpec(memory_space=pl.ANY),
                      pl.BlockSpec((TILE_C, D), row),
                      pl.BlockSpec((8, TILE_C), lambda i, p: (0, i))]
                     + [pl.BlockSpec(c.shape, lambda i, p, nd=c.ndim: (0,) * nd) for c in consts],
            out_specs=[pl.BlockSpec((TILE_C, D), row),
                       pl.BlockSpec((TILE_C, 2 * KVW), row),
                       pl.BlockSpec((TILE_C, D), row)],
            scratch_shapes=[pltpu.VMEM((2, 2, TILE_C, D), F32), pltpu.SemaphoreType.DMA((2,))],
        ),
        out_shape=[jax.ShapeDtypeStruct((T, D), F32),
                   jax.ShapeDtypeStruct((T, 2 * KVW), F32),
                   jax.ShapeDtypeStruct((T, D), F32)],
        compiler_params=_cparams(),
        name="combine_kvq",
    )(pos, ys, x1, meta, *consts)


def _combine_final_body(pos_ref, y_hbm, x_ref, meta_ref, gf_ref, yp_ref, ys_ref, ybuf, sem):
    i = pl.program_id(0)
    out = _rms(_combined(pos_ref, y_hbm, x_ref, meta_ref, ybuf, sem), gf_ref[...])

    @pl.when(i < TP // TILE_C)
    def _():
        yp_ref[...] = out

    @pl.when(i >= TP // TILE_C)
    def _():
        ys_ref[...] = out


def _combine_final(pos, ys, x3, meta, gf):
    row = lambda i, p: (i, 0)
    npt = TP // TILE_C
    return pl.pallas_call(
        _combine_final_body,
        grid_spec=pltpu.PrefetchScalarGridSpec(
            num_scalar_prefetch=1,
            grid=(T // TILE_C,),
            in_specs=[pl.BlockSpec(memory_space=pl.ANY),
                      pl.BlockSpec((TILE_C, D), row),
                      pl.BlockSpec((8, TILE_C), lambda i, p: (0, i)),
                      pl.BlockSpec((1, D), lambda i, p: (0, 0))],
            out_specs=[pl.BlockSpec((TILE_C, D), lambda i, p: (jnp.minimum(i, npt - 1), 0)),
                       pl.BlockSpec((TILE_C, D), lambda i, p: (jnp.maximum(i - npt, 0), 0))],
            scratch_shapes=[pltpu.VMEM((2, 2, TILE_C, D), F32), pltpu.SemaphoreType.DMA((2,))],
        ),
        out_shape=[jax.ShapeDtypeStruct((TP, D), F32), jax.ShapeDtypeStruct((TS, D), F32)],
        compiler_params=_cparams(),
        name="combine_final",
    )(pos, ys, x3, meta, gf)


ATT_SUB = 2
TILE_A = ATT_SUB * WINDOW
NB_ATT = SEQ // TILE_A
NP_ATT = TP // TILE_A


def _attn_mixer_body(q_ref, kvc_ref, kvp_ref, x_ref, os_ref, bm_ref, band_ref, sink_ref,
                     wo_ref, bo_ref, gffn_ref, a_ref, rbias_ref, u_ref,
                     x3_ref, h_ref, meta_ref, cnt_out_ref, obuf, cnt_scr):
    i = pl.program_id(0)
    is_prompt = i < NP_ATT
    j = i % NB_ATT

    @pl.when(i == 0)
    def _():
        cnt_scr[...] = jnp.zeros_like(cnt_scr)

    @pl.when(is_prompt)
    def _():
        kall = jnp.concatenate([kvp_ref[:, :KVW], kvc_ref[:, :KVW]], axis=0).astype(BF16)
        vall = jnp.concatenate([kvp_ref[:, KVW:], kvc_ref[:, KVW:]], axis=0).astype(BF16)
        bm = bm_ref[...]
        for sb in range(ATT_SUB):
            kb = kall[sb * WINDOW:(sb + 2) * WINDOW]
            vb = vall[sb * WINDOW:(sb + 2) * WINDOW]
            first = jnp.minimum(j, 1) if sb == 0 else 1
            valid = band_ref[first] > 0.0
            q = q_ref[sb * WINDOW:(sb + 1) * WINDOW, :]
            for r in range(GQA):
                qr = q[:, r * KVW:(r + 1) * KVW]
                qm = (jnp.concatenate([qr] * N_KV, axis=0) * bm).astype(BF16)
                s = jnp.where(valid, _dot_nt(qm, kb), MASK_VALUE)
                sink = sink_ref[r * N_KV * WINDOW:(r + 1) * N_KV * WINDOW]
                m = jnp.maximum(jnp.max(s, axis=-1, keepdims=True), sink)
                p = jnp.exp(s - m)
                den = jnp.sum(p, axis=-1, keepdims=True) + jnp.exp(sink - m)
                o = _dot(p.astype(BF16), vb) * (bm * (1.0 / den))
                acc = o[0:WINDOW]
                for g in range(1, N_KV):
                    acc = acc + o[g * WINDOW:(g + 1) * WINDOW]
                obuf[sb * WINDOW:(sb + 1) * WINDOW, r * KVW:(r + 1) * KVW] = acc

    @pl.when(jnp.logical_not(is_prompt))
    def _():
        obuf[...] = os_ref[...]

    x3 = x_ref[...] + _dot(obuf[...].astype(BF16), wo_ref[...]) + bo_ref[...]
    x3_ref[...] = x3
    h, meta = _moe_prologue(x3, gffn_ref[...], a_ref, rbias_ref, u_ref, cnt_scr)
    h_ref[...] = h
    meta_ref[...] = meta
    cnt_out_ref[...] = cnt_scr[...]


def _attn_mixer(q, kv, x2, o_s, bm, band, sink_col, wo, bo, gffn, a, rbias, u):
    consts = [bm, band, sink_col, wo, bo, gffn, a, rbias, u]
    prow = lambda i: (jnp.minimum(i, NP_ATT - 1), 0)

    def prev(i):
        ic = jnp.minimum(i, NP_ATT - 1)
        return (ATT_SUB * ic - jnp.where(ic % NB_ATT == 0, 0, 1), 0)

    row = lambda i: (i, 0)
    return pl.pallas_call(
        _attn_mixer_body,
        grid=(T // TILE_A,),
        in_specs=[pl.BlockSpec((TILE_A, D), prow),
                  pl.BlockSpec((TILE_A, 2 * KVW), prow),
                  pl.BlockSpec((WINDOW, 2 * KVW), prev),
                  pl.BlockSpec((TILE_A, D), row),
                  pl.BlockSpec((TILE_A, D), lambda i: (jnp.maximum(i - NP_ATT, 0), 0))]
                 + [_const_spec(c.shape) for c in consts],
        out_specs=[pl.BlockSpec((TILE_A, D), row),
                   pl.BlockSpec((TILE_A, D), row),
                   pl.BlockSpec((8, TILE_A), lambda i: (0, i)),
                   _const_spec((N_EXPERTS, 128))],
        out_shape=[jax.ShapeDtypeStruct((T, D), F32),
                   jax.ShapeDtypeStruct((T, D), F32),
                   jax.ShapeDtypeStruct((8, T), F32),
                   jax.ShapeDtypeStruct((N_EXPERTS, 128), F32)],
        scratch_shapes=[pltpu.VMEM((TILE_A, D), F32), pltpu.VMEM((N_EXPERTS, 128), F32)],
        compiler_params=_cparams(),
        name="attn_mixer",
    )(q, kv, kv, x2, o_s, *consts)


def _attn_sample_body(q_ref, kvn_ref, ck_ref, cv_ref, bm_ref, vc_ref, vn_ref, sink_ref,
                      o_ref, nk_ref, nv_ref):
    bm = bm_ref[...]
    valid_c = vc_ref[...] > 0.0
    valid_n = vn_ref[...] > 0.0
    sink = sink_ref[...]
    hist = WINDOW - DEC_SEQ

    def one(b, carry):
        q8 = q_ref[b]
        kvn = kvn_ref[b]
        qm = jnp.concatenate(
            [q8[:, r * KVW:(r + 1) * KVW] for r in range(GQA) for _ in range(N_KV)], axis=0)
        qm = (qm * bm).astype(BF16)
        s_c = jnp.where(valid_c, _dot_nt(qm, ck_ref[b].astype(BF16)), MASK_VALUE)
        s_n = jnp.where(valid_n, _dot_nt(qm, kvn[:, :KVW].astype(BF16)), MASK_VALUE)
        m = jnp.maximum(jnp.maximum(jnp.max(s_c, axis=-1, keepdims=True),
                                    jnp.max(s_n, axis=-1, keepdims=True)), sink)
        p_c = jnp.exp(s_c - m)
        p_n = jnp.exp(s_n - m)
        den = (jnp.sum(p_c, axis=-1, keepdims=True) + jnp.sum(p_n, axis=-1, keepdims=True)
               + jnp.exp(sink - m))
        o = (_dot(p_c.astype(BF16), cv_ref[b].astype(BF16))
             + _dot(p_n.astype(BF16), kvn[:, KVW:].astype(BF16))) * (bm * (1.0 / den))
        outs = []
        for r in range(GQA):
            acc = o[r * 32:r * 32 + 8]
            for g in range(1, N_KV):
                acc = acc + o[r * 32 + g * 8:r * 32 + g * 8 + 8]
            outs.append(acc)
        o_ref[b] = jnp.concatenate(outs, axis=1)
        nk_ref[b, 0:hist, :] = ck_ref[b, DEC_SEQ:WINDOW, :]
        nk_ref[b, hist:WINDOW, :] = kvn_ref[b, 0:DEC_SEQ, 0:KVW]
        nv_ref[b, 0:hist, :] = cv_ref[b, DEC_SEQ:WINDOW, :]
        nv_ref[b, hist:WINDOW, :] = kvn_ref[b, 0:DEC_SEQ, KVW:2 * KVW]
        return carry

    lax.fori_loop(0, SAMPLE_BB, one, 0, unroll=2)


def _attn_sample(q8, kvn16, ck, cv, bm, vc, vn, sink_col8):
    bb = SAMPLE_BB
    blk = lambda i: (i, 0, 0)
    consts = [bm, vc, vn, sink_col8]
    return pl.pallas_call(
        _attn_sample_body,
        grid=(DEC_BATCH // bb,),
        in_specs=[pl.BlockSpec((bb, 8, D), blk),
                  pl.BlockSpec((bb, NKN, 2 * KVW), blk),
                  pl.BlockSpec((bb, WINDOW, KVW), blk),
                  pl.BlockSpec((bb, WINDOW, KVW), blk)] + [_const_spec(c.shape) for c in consts],
        out_specs=[pl.BlockSpec((bb, 8, D), blk),
                   pl.BlockSpec((bb, WINDOW, KVW), blk),
                   pl.BlockSpec((bb, WINDOW, KVW), blk)],
        out_shape=[jax.ShapeDtypeStruct((DEC_BATCH, 8, D), F32),
                   jax.ShapeDtypeStruct((DEC_BATCH, WINDOW, KVW), F32),
                   jax.ShapeDtypeStruct((DEC_BATCH, WINDOW, KVW), F32)],
        compiler_params=_cparams(),
        name="attn_sample",
    )(q8, kvn16, ck, cv, *consts)


def _router_weights(w_group, b_group, w_router, b_router):
    wt = jnp.zeros((N_LOGIT_ROWS, D), F32)
    wt = wt.at[0:N_GROUPS].set(w_group.T)
    wt = wt.at[8:8 + N_EXPERTS].set(jnp.transpose(w_router, (0, 2, 1)).reshape(N_EXPERTS, D))
    hi = wt.astype(BF16)
    lo = (wt - hi.astype(F32)).astype(BF16)
    a = jnp.concatenate([hi, lo], axis=0)
    bias = jnp.zeros((N_LOGIT_ROWS,), F32)
    bias = bias.at[0:N_GROUPS].set(b_group)
    bias = bias.at[N_GROUPS:8].set(MASK_VALUE)
    bias = bias.at[8:8 + N_EXPERTS].set(b_router.reshape(N_EXPERTS))
    return a, jnp.broadcast_to(bias[:, None], (N_LOGIT_ROWS, 128))


def _routing_tables(cnt):
    counts = cnt[:, 0].astype(jnp.int32)
    padded = ((counts + TILE_M - 1) // TILE_M) * TILE_M
    ends = jnp.cumsum(padded)
    offs = ends - padded
    tile_start = jnp.arange(N_TILES, dtype=jnp.int32) * TILE_M
    n_used = (ends[-1] // TILE_M).astype(jnp.int32)
    tile_e = jnp.sum(tile_start[:, None] >= ends[None, :], axis=1).astype(jnp.int32)
    last_e = tile_e[jnp.maximum(n_used - 1, 0)]
    tile_e = jnp.where(tile_start < ends[-1], tile_e, last_e)
    zstart = jnp.concatenate([(offs + counts) // 8 * 8, ends[-1:]])
    zend = jnp.concatenate([ends, jnp.full((1,), P_ROWS, jnp.int32)])
    pad = jnp.concatenate([zstart, (zend - zstart) // 8]).astype(jnp.int32)
    return offs.astype(jnp.int32), pad, tile_e, n_used.reshape(1)


def _moe_layer(layer, h, meta, cnt, w1, w3, w2):
    offs, pad, tile_e, n_used = _routing_tables(cnt)
    pos = _positions(offs, meta).reshape(N_ASSIGN)
    xs = _scatter_rows(pos, pad, h)
    ys = _expert_ffn(layer, tile_e, n_used, xs, w1, w3, w2)
    return pos, ys


def _head_perm():
    idx = jnp.arange(D).reshape(N_KV, GQA, HEAD_DIM)
    return jnp.transpose(idx, (1, 0, 2)).reshape(D)


def _block_mask(rows_per_block, n_rep):
    n = n_rep * N_KV * rows_per_block
    r = (jnp.arange(n)[:, None] // rows_per_block) % N_KV
    c = jnp.arange(KVW)[None, :] // HEAD_DIM
    return (r == c).astype(F32)


def kernel(x_prompt, x_sample, state_conv, cache_k, cache_v, norm_mix, norm_ffn, conv_w_in, conv_b_in, conv_w_dw, conv_b_dw, conv_ln_g, conv_ln_b, conv_w_out, conv_b_out, norm_kv, w_kv, b_kv, attn_w_q, attn_b_q, attn_sinks, attn_w_o, attn_b_o, moe_w_group, moe_b_group, moe_w_router, moe_b_router, moe_w1, moe_w3, moe_w2, norm_final):
    r2 = lambda v: v.reshape(1, -1)
    u_tri = jnp.triu(jnp.ones((TILE_SEQ, TILE_SEQ), BF16), 1)
    ar0, rb0 = _router_weights(moe_w_group[0], moe_b_group[0], moe_w_router[0], moe_b_router[0])
    ar1, rb1 = _router_weights(moe_w_group[1], moe_b_group[1], moe_w_router[1], moe_b_router[1])

    win = conv_w_in[0].astype(BF16)
    wout = conv_w_out[0].astype(BF16)
    wdw = conv_w_dw[0]
    xs2 = x_sample.reshape(TS, D)
    u_s = _glu_sample(xs2, r2(norm_mix[0]), win, r2(conv_b_in[0]))
    tt = jnp.arange(DEC_SEQ)[:, None]
    jh = jnp.arange(CONV_HIST)[None, :]
    wh = jnp.where((jh >= tt)[..., None], wdw[jnp.clip(jh - tt, 0, CONV_W - 1)], 0.0)
    ju = jnp.arange(DEC_SEQ)[None, :]
    wu = jnp.where((ju <= tt)[..., None], wdw[jnp.clip(CONV_HIST - tt + ju, 0, CONV_W - 1)], 0.0)
    y_s, state_s = _dwconv_sample(u_s.reshape(DEC_BATCH, DEC_SEQ, D), state_conv, wh, wu)
    wdw8 = jnp.broadcast_to(wdw[:, None, :], (CONV_W, 8, D))
    x1, h, meta, cnt, state_p = _conv_mixer(
        x_prompt.reshape(TP, D), xs2, y_s.reshape(TS, D), r2(norm_mix[0]), win, r2(conv_b_in[0]), wdw8,
        r2(conv_b_dw[0]), r2(conv_ln_g[0]), r2(conv_ln_b[0]), wout, r2(conv_b_out[0]), r2(norm_ffn[0]),
        ar0, rb0, u_tri)

    pos, ys = _moe_layer(0, h, meta, cnt, moe_w1, moe_w3, moe_w2)
    perm = _head_perm()
    scale = HEAD_DIM ** -0.5
    wq = (attn_w_q[0][:, perm] * scale).astype(BF16)
    bq = r2(attn_b_q[0][perm] * scale)
    wo = attn_w_o[0][perm, :].astype(BF16)
    x2, kv, q = _combine_kvq(pos, ys, x1, meta, r2(norm_kv), w_kv.astype(BF16), r2(b_kv),
                             r2(norm_mix[1]), wq, bq)

    sinks = attn_sinks[0].astype(F32).reshape(N_KV, GQA).T
    sink_col = jnp.repeat(sinks.reshape(-1), WINDOW)[:, None]
    sink_col8 = jnp.repeat(sinks.reshape(-1), 8)[:, None]
    qi = (jnp.arange(GQA * N_KV * 8) % 8)[:, None]
    vc = ((qi < DEC_SEQ) & (jnp.arange(WINDOW)[None, :] > qi)).astype(F32)
    vn = ((qi < DEC_SEQ) & (jnp.arange(NKN)[None, :] <= qi)).astype(F32)
    q8 = jnp.pad(q[TP:].reshape(DEC_BATCH, DEC_SEQ, D), ((0, 0), (0, 8 - DEC_SEQ), (0, 0)))
    kvn = jnp.pad(kv[TP:].reshape(DEC_BATCH, DEC_SEQ, 2 * KVW), ((0, 0), (0, NKN - DEC_SEQ), (0, 0)))
    o_s8, nk_s, nv_s = _attn_sample(q8, kvn, cache_k.reshape(DEC_BATCH, WINDOW, KVW),
                                    cache_v.reshape(DEC_BATCH, WINDOW, KVW),
                                    _block_mask(8, GQA), vc, vn, sink_col8)
    o_s = o_s8[:, :DEC_SEQ].reshape(TS, D)
    pq = (jnp.arange(N_KV * WINDOW) % WINDOW)[:, None]
    pk = jnp.arange(2 * WINDOW)[None, :]
    dist = pq + WINDOW - pk
    band_rest = (dist >= 0) & (dist < WINDOW)
    band = jnp.stack([band_rest & (pk >= WINDOW), band_rest]).astype(F32)
    x3, h, meta, cnt = _attn_mixer(q, kv, x2, o_s, _block_mask(WINDOW, 1), band, sink_col, wo,
                                   r2(attn_b_o[0]), r2(norm_ffn[1]), ar1, rb1, u_tri[:TILE_A, :TILE_A])

    pos, ys = _moe_layer(1, h, meta, cnt, moe_w1, moe_w3, moe_w2)
    y_p, y_s2 = _combine_final(pos, ys, x3, meta, r2(norm_final))

    kvp = kv[:TP].reshape(BATCH, SEQ, 2 * KVW)[:, SEQ - WINDOW:]
    new_k_p = kvp[..., :KVW].reshape(BATCH, WINDOW, N_KV, HEAD_DIM)
    new_v_p = kvp[..., KVW:].reshape(BATCH, WINDOW, N_KV, HEAD_DIM)
    return (y_p.reshape(BATCH, SEQ, D), y_s2.reshape(DEC_BATCH, DEC_SEQ, D),
            state_p[None], state_s, new_k_p, new_v_p,
            nk_s.reshape(DEC_BATCH, WINDOW, N_KV, HEAD_DIM), nv_s.reshape(DEC_BATCH, WINDOW, N_KV, HEAD_DIM))
```

```python
import jax
import jax.numpy as jnp
from jax import lax
from jax.experimental import pallas as pl
from jax.experimental.pallas import tpu as pltpu

D = 1024
BATCH = 8
SEQ = 2048
DEC_BATCH = 128
DEC_SEQ = 4
CONV_W = 31
CONV_HIST = CONV_W - 1
HEAD_DIM = 64
N_HEADS = 16
N_KV = 4
GQA = 4
KVW = N_KV * HEAD_DIM
WINDOW = 128
N_GROUPS = 4
EPG = 8
N_EXPERTS = 32
D_EXPERT = 512
RMS_EPS = 1e-5
LN_EPS = 1e-5
MASK_VALUE = -1e30

TP = BATCH * SEQ
TS = DEC_BATCH * DEC_SEQ
T = TP + TS
N_ASSIGN = 2 * T

TILE_SEQ = 256
TILE_M = 256
N_TILES = (N_ASSIGN + N_EXPERTS * (TILE_M - 1)) // TILE_M + 1
P_ROWS = N_TILES * TILE_M
TILE_C = 256
TILE_SC = 1536
N_LOGIT_ROWS = 48
SAMPLE_BB = 16
NKN = 16
ISSUE_UNROLL = 8
VMEM_LIMIT = 56 * 1024 * 1024

F32 = jnp.float32
BF16 = jnp.bfloat16


def _cparams(n_axes=1):
    return pltpu.CompilerParams(dimension_semantics=("arbitrary",) * n_axes,
                                vmem_limit_bytes=VMEM_LIMIT)


def _const_spec(shape):
    nd = len(shape)
    return pl.BlockSpec(shape, lambda *_: (0,) * nd)


def _rms(x, g):
    return x * lax.rsqrt(jnp.mean(x * x, axis=-1, keepdims=True) + RMS_EPS) * g


def _dot(a, b):
    return jnp.dot(a, b, preferred_element_type=F32)


def _dot_nt(a, b):
    return lax.dot_general(a, b, (((1,), (1,)), ((), ())), preferred_element_type=F32)


def _moe_prologue(x, g_ffn, a_ref, rbias_ref, u_ref, cnt_ref):
    n = x.shape[0]
    h = _rms(x, g_ffn)
    h_hi = h.astype(BF16)
    h_lo = (h - h_hi.astype(F32)).astype(BF16)
    a = a_ref[...]
    l1 = _dot_nt(a, h_hi)
    l2 = _dot_nt(a[:N_LOGIT_ROWS], h_lo)
    logits = l1[:N_LOGIT_ROWS] + l1[N_LOGIT_ROWS:] + l2 + rbias_ref[:, 0:1]
    iota8 = lax.broadcasted_iota(jnp.int32, (8, n), 0).astype(F32)
    gl = logits[0:8]
    gmax = jnp.max(gl, axis=0, keepdims=True)
    g_idx = jnp.min(jnp.where(gl == gmax, iota8, 8.0), axis=0, keepdims=True)
    g_w = 1.0 / jnp.sum(jnp.exp(gl - gmax), axis=0, keepdims=True)
    es = jnp.where(g_idx == 0.0, logits[8:16],
                   jnp.where(g_idx == 1.0, logits[16:24],
                             jnp.where(g_idx == 2.0, logits[24:32], logits[32:40])))
    v1 = jnp.max(es, axis=0, keepdims=True)
    i1 = jnp.min(jnp.where(es == v1, iota8, 8.0), axis=0, keepdims=True)
    es2 = jnp.where(iota8 == i1, -jnp.inf, es)
    v2 = jnp.max(es2, axis=0, keepdims=True)
    i2 = jnp.min(jnp.where(es2 == v2, iota8, 8.0), axis=0, keepdims=True)
    e2x = jnp.exp(v2 - v1)
    w1 = g_w / (1.0 + e2x)
    w2 = g_w * e2x / (1.0 + e2x)
    ex1 = g_idx * EPG + i1
    ex2 = g_idx * EPG + i2
    iota32 = lax.broadcasted_iota(jnp.int32, (N_EXPERTS, n), 0).astype(F32)
    oh1 = jnp.where(iota32 == ex1, 1.0, 0.0)
    oh2 = jnp.where(iota32 == ex2, 1.0, 0.0)
    onehot = oh1 + oh2
    before = _dot(onehot.astype(BF16), u_ref[...]) + cnt_ref[:, 0:1]
    rank1 = jnp.sum(oh1 * before, axis=0, keepdims=True)
    rank2 = jnp.sum(oh2 * before, axis=0, keepdims=True)
    cnt_ref[...] = cnt_ref[...] + jnp.sum(onehot, axis=1, keepdims=True)
    rows = (ex1, ex2, w1, w2, rank1, rank2)
    meta = jnp.zeros((8, n), F32)
    for r, val in enumerate(rows):
        meta = jnp.where(iota8 == r, val, meta)
    return h, meta


def _ln_silu(y, g, b):
    mu = jnp.mean(y, axis=-1, keepdims=True)
    yc = y - mu
    var = jnp.mean(yc * yc, axis=-1, keepdims=True)
    z = yc * lax.rsqrt(var + LN_EPS) * g + b
    return z * jax.nn.sigmoid(z)


CONV_CH = 32
NP_CONV = TP // TILE_SEQ
NSQ = SEQ // TILE_SEQ


def _conv_mixer_body(xp_ref, xs_ref, ys_ref, gmix_ref, win_ref, bin_ref, wdw_ref, bdw_ref,
                     lng_ref, lnb_ref, wout_ref, bout_ref, gffn_ref, a_ref, rbias_ref, u_ref,
                     x1_ref, h_ref, meta_ref, cnt_out_ref, state_ref,
                     ubuf, ush, ybuf, cnt_scr):
    i = pl.program_id(0)
    is_prompt = i < NP_CONV
    s = i % NSQ

    @pl.when(i == 0)
    def _():
        cnt_scr[...] = jnp.zeros_like(cnt_scr)

    @pl.when(is_prompt)
    def _():
        @pl.when(s == 0)
        def _():
            ubuf[0:32, :] = jnp.zeros((32, D), F32)

        hn = _rms(xp_ref[...], gmix_ref[...])
        u2 = _dot(hn.astype(BF16), win_ref[...]) + bin_ref[...]
        ubuf[32:32 + TILE_SEQ, :] = u2[:, :D] * jax.nn.sigmoid(u2[:, D:])

        for sh in range(1, 8):
            ush[sh - 1] = ubuf[sh:sh + TILE_SEQ + 24, :]

        ngrp = CONV_CH // 8

        def chunk(c, carry):
            r0 = pl.multiple_of(c * CONV_CH, CONV_CH)
            for lt in range(D // 128):
                lanes = slice(lt * 128, (lt + 1) * 128)
                acc = [None] * ngrp
                for sh in range(8):
                    a8s = [a8 for a8 in range(5) if 0 <= 8 * a8 + sh - 2 < CONV_W]
                    win = {}
                    for gi in range(a8s[0], a8s[-1] + ngrp):
                        start = pl.multiple_of(r0 + 8 * gi, 8)
                        if sh == 0:
                            win[gi] = ubuf[pl.ds(start, 8), lanes]
                        else:
                            win[gi] = ush[sh - 1, pl.ds(start, 8), lanes]
                    for a8 in a8s:
                        w = wdw_ref[8 * a8 + sh - 2, :, lanes]
                        for gq in range(ngrp):
                            term = w * win[a8 + gq]
                            acc[gq] = term if acc[gq] is None else acc[gq] + term
                for gq in range(ngrp):
                    ybuf[pl.ds(pl.multiple_of(r0 + 8 * gq, 8), 8), lanes] = acc[gq]
            return carry

        lax.fori_loop(0, TILE_SEQ // CONV_CH, chunk, 0)

        @pl.when(s == NSQ - 1)
        def _():
            state_ref[0] = ubuf[TILE_SEQ + 2:TILE_SEQ + 32, :]

        ubuf[0:32, :] = ubuf[TILE_SEQ:TILE_SEQ + 32, :]

    @pl.when(jnp.logical_not(is_prompt))
    def _():
        ybuf[...] = ys_ref[...]

    x = jnp.where(is_prompt, xp_ref[...], xs_ref[...])
    act = _ln_silu(ybuf[...] + bdw_ref[...], lng_ref[...], lnb_ref[...])
    x1 = x + _dot(act.astype(BF16), wout_ref[...]) + bout_ref[...]
    x1_ref[...] = x1
    h, meta = _moe_prologue(x1, gffn_ref[...], a_ref, rbias_ref, u_ref, cnt_scr)
    h_ref[...] = h
    meta_ref[...] = meta
    cnt_out_ref[...] = cnt_scr[...]


def _conv_mixer(xp, xs, ys, gmix, win, b_in, wdw, bdw, lng, lnb, wout, bout, gffn, a, rbias, u):
    consts = [gmix, win, b_in, wdw, bdw, lng, lnb, wout, bout, gffn, a, rbias, u]
    prow = lambda i: (jnp.minimum(i, NP_CONV - 1), 0)
    srow = lambda i: (jnp.maximum(i - NP_CONV, 0), 0)
    row = lambda i: (i, 0)
    return pl.pallas_call(
        _conv_mixer_body,
        grid=(T // TILE_SEQ,),
        in_specs=[pl.BlockSpec((TILE_SEQ, D), prow),
                  pl.BlockSpec((TILE_SEQ, D), srow),
                  pl.BlockSpec((TILE_SEQ, D), srow)] + [_const_spec(c.shape) for c in consts],
        out_specs=[
            pl.BlockSpec((TILE_SEQ, D), row),
            pl.BlockSpec((TILE_SEQ, D), row),
            pl.BlockSpec((8, TILE_SEQ), lambda i: (0, i)),
            _const_spec((N_EXPERTS, 128)),
            pl.BlockSpec((1, CONV_HIST, D), lambda i: (jnp.minimum(i // NSQ, BATCH - 1), 0, 0)),
        ],
        out_shape=[
            jax.ShapeDtypeStruct((T, D), F32),
            jax.ShapeDtypeStruct((T, D), F32),
            jax.ShapeDtypeStruct((8, T), F32),
            jax.ShapeDtypeStruct((N_EXPERTS, 128), F32),
            jax.ShapeDtypeStruct((BATCH, CONV_HIST, D), F32),
        ],
        scratch_shapes=[
            pltpu.VMEM((TILE_SEQ + 32, D), F32),
            pltpu.VMEM((7, TILE_SEQ + 24, D), F32),
            pltpu.VMEM((TILE_SEQ, D), F32),
            pltpu.VMEM((N_EXPERTS, 128), F32),
        ],
        compiler_params=_cparams(),
        name="conv_mixer",
    )(xp, xs, ys, *consts)


def _glu_sample_body(x_ref, gmix_ref, win_ref, bin_ref, u_ref):
    hn = _rms(x_ref[...], gmix_ref[...])
    u2 = _dot(hn.astype(BF16), win_ref[...]) + bin_ref[...]
    u_ref[...] = u2[:, :D] * jax.nn.sigmoid(u2[:, D:])


def _glu_sample(x, gmix, win, b_in):
    args = [x, gmix, win, b_in]
    return pl.pallas_call(
        _glu_sample_body,
        grid=(1,),
        in_specs=[_const_spec(a.shape) for a in args],
        out_specs=_const_spec((TS, D)),
        out_shape=jax.ShapeDtypeStruct((TS, D), F32),
        compiler_params=_cparams(),
        name="glu_sample",
    )(*args)


def _dwconv_sample_body(u_ref, st_ref, wh_ref, wu_ref, y_ref, nst_ref):
    st = st_ref[0]
    u = u_ref[...]
    for t in range(DEC_SEQ):
        y_ref[:, t:t + 1, :] = (jnp.sum(st * wh_ref[t][None], axis=1, keepdims=True)
                                + jnp.sum(u * wu_ref[t][None], axis=1, keepdims=True))
    nst_ref[0, :, 0:CONV_HIST - DEC_SEQ, :] = st_ref[0, :, DEC_SEQ:CONV_HIST, :]
    nst_ref[0, :, CONV_HIST - DEC_SEQ:CONV_HIST, :] = u


def _dwconv_sample(u3, state, wh, wu):
    bb = SAMPLE_BB
    blk = lambda i: (i, 0, 0)
    sblk = lambda i: (0, i, 0, 0)
    return pl.pallas_call(
        _dwconv_sample_body,
        grid=(DEC_BATCH // bb,),
        in_specs=[
            pl.BlockSpec((bb, DEC_SEQ, D), blk),
            pl.BlockSpec((1, bb, CONV_HIST, D), sblk),
            _const_spec(wh.shape), _const_spec(wu.shape),
        ],
        out_specs=[pl.BlockSpec((bb, DEC_SEQ, D), blk), pl.BlockSpec((1, bb, CONV_HIST, D), sblk)],
        out_shape=[
            jax.ShapeDtypeStruct((DEC_BATCH, DEC_SEQ, D), F32),
            jax.ShapeDtypeStruct((1, DEC_BATCH, CONV_HIST, D), F32),
        ],
        compiler_params=_cparams(),
        name="dwconv_sample",
    )(u3, state, wh, wu)


def _positions_body(offs_ref, meta_ref, pos_ref):
    ex = meta_ref[0:2, :]
    acc = meta_ref[4:6, :]
    for e in range(N_EXPERTS):
        acc = acc + jnp.where(ex == float(e), offs_ref[e].astype(F32), 0.0)
    pos_ref[...] = acc.astype(jnp.int32)


def _positions(offs, meta):
    return pl.pallas_call(
        _positions_body,
        grid_spec=pltpu.PrefetchScalarGridSpec(
            num_scalar_prefetch=1,
            grid=(1,),
            in_specs=[pl.BlockSpec((8, T), lambda i, o: (0, 0))],
            out_specs=pl.BlockSpec((2, T), lambda i, o: (0, 0)),
        ),
        out_shape=jax.ShapeDtypeStruct((2, T), jnp.int32),
        compiler_params=_cparams(),
        name="positions",
    )(offs, meta)


def _scatter_body(pos_ref, pad_ref, h_ref, xs_hbm, zbuf, sem):
    i = pl.program_id(0)

    @pl.when(i == 0)
    def _():
        zbuf[...] = jnp.zeros_like(zbuf)

        def per_region(e, total):
            zstart = pad_ref[e]
            nchunk = pad_ref[N_EXPERTS + 1 + e]

            def zissue(r, carry):
                dst = xs_hbm.at[pl.ds(pl.multiple_of(zstart + 8 * r, 8), 8), :]
                pltpu.make_async_copy(zbuf, dst, sem).start()
                return carry

            lax.fori_loop(0, nchunk, zissue, 0)
            return total + nchunk

        total = lax.fori_loop(0, N_EXPERTS + 1, per_region, jnp.int32(0))

        @pl.when(total > 0)
        def _():
            nrows = pl.multiple_of(total * 8, 8)
            pltpu.make_async_copy(xs_hbm.at[pl.ds(0, nrows), :],
                                  xs_hbm.at[pl.ds(0, nrows), :], sem).wait()

    @pl.when(i > 0)
    def _():
        base = (i - 1) * TILE_SC

        def issue(c, carry):
            for j in range(ISSUE_UNROLL):
                t = c * ISSUE_UNROLL + j
                src = h_ref.at[pl.ds(t, 1), :]
                for k in range(2):
                    dst = xs_hbm.at[pl.ds(pos_ref[k * T + base + t], 1), :]
                    pltpu.make_async_copy(src, dst, sem).start(priority=k)
            return carry

        lax.fori_loop(0, TILE_SC // ISSUE_UNROLL, issue, 0)
        for _ in range(2):
            pltpu.make_async_copy(h_ref, xs_hbm.at[pl.ds(0, TILE_SC), :], sem).wait()


def _scatter_rows(pos, pad, h):
    return pl.pallas_call(
        _scatter_body,
        grid_spec=pltpu.PrefetchScalarGridSpec(
            num_scalar_prefetch=2,
            grid=(1 + T // TILE_SC,),
            in_specs=[pl.BlockSpec((TILE_SC, D), lambda i, p, q: (jnp.maximum(i - 1, 0), 0))],
            out_specs=pl.BlockSpec(memory_space=pl.ANY),
            scratch_shapes=[pltpu.VMEM((8, D), F32), pltpu.SemaphoreType.DMA(())],
        ),
        out_shape=jax.ShapeDtypeStruct((P_ROWS, D), F32),
        compiler_params=_cparams(),
        name="scatter_rows",
    )(pos, pad, h)


def _expert_body(te_ref, nu_ref, x_ref, w1_ref, w3_ref, w2_ref, y_ref, w1b, w3b, w2b):
    i = pl.program_id(0)
    live = i < nu_ref[0]
    prev = te_ref[jnp.maximum(i - 1, 0)]
    fresh = (i == 0) | (te_ref[i] != prev)

    @pl.when(live & fresh)
    def _():
        w1b[...] = w1_ref[0, 0].astype(BF16)
        w3b[...] = w3_ref[0, 0].astype(BF16)
        w2b[...] = w2_ref[0, 0].astype(BF16)

    @pl.when(live)
    def _():
        xb = x_ref[...].astype(BF16)
        a = _dot(xb, w1b[...])
        g = _dot(xb, w3b[...])
        act = a * jax.nn.sigmoid(a) * g
        y_ref[...] = _dot(act.astype(BF16), w2b[...])

    @pl.when(jnp.logical_not(live))
    def _():
        y_ref[...] = jnp.zeros_like(y_ref)


def _expert_ffn(layer, tile_e, n_used, xs, w1, w3, w2):
    xrow = lambda i, te, nu: (jnp.minimum(i, nu[0] - 1), 0)
    wsel = lambda i, te, nu: (layer, te[i], 0, 0)
    return pl.pallas_call(
        _expert_body,
        grid_spec=pltpu.PrefetchScalarGridSpec(
            num_scalar_prefetch=2,
            grid=(N_TILES,),
            in_specs=[
                pl.BlockSpec((TILE_M, D), xrow),
                pl.BlockSpec((1, 1, D, D_EXPERT), wsel),
                pl.BlockSpec((1, 1, D, D_EXPERT), wsel),
                pl.BlockSpec((1, 1, D_EXPERT, D), wsel),
            ],
            out_specs=pl.BlockSpec((TILE_M, D), lambda i, te, nu: (i, 0)),
            scratch_shapes=[
                pltpu.VMEM((D, D_EXPERT), BF16),
                pltpu.VMEM((D, D_EXPERT), BF16),
                pltpu.VMEM((D_EXPERT, D), BF16),
            ],
        ),
        out_shape=jax.ShapeDtypeStruct((P_ROWS, D), F32),
        compiler_params=_cparams(),
        name="expert_ffn",
    )(tile_e, n_used, xs, w1, w3, w2)


def _gather_issue(pos_ref, y_hbm, ybuf, sem, tile, slot):
    base = tile * TILE_C

    def issue(c, carry):
        for j in range(ISSUE_UNROLL):
            t = c * ISSUE_UNROLL + j
            for k in range(2):
                pltpu.make_async_copy(y_hbm.at[pl.ds(pos_ref[k * T + base + t], 1), :],
                                      ybuf.at[slot, k, pl.ds(t, 1), :], sem.at[slot]).start(priority=k)
        return carry

    lax.fori_loop(0, TILE_C // ISSUE_UNROLL, issue, 0)


def _gather_wait(y_hbm, ybuf, sem, slot):
    for k in range(2):
        pltpu.make_async_copy(y_hbm.at[pl.ds(0, TILE_C), :], ybuf.at[slot, k], sem.at[slot]).wait()


def _combined(pos_ref, y_hbm, x_ref, meta_ref, ybuf, sem):
    i = pl.program_id(0)
    n = pl.num_programs(0)
    slot = i % 2

    @pl.when(i == 0)
    def _():
        _gather_issue(pos_ref, y_hbm, ybuf, sem, 0, 0)

    @pl.when(i + 1 < n)
    def _():
        _gather_issue(pos_ref, y_hbm, ybuf, sem, i + 1, 1 - slot)

    _gather_wait(y_hbm, ybuf, sem, slot)
    mt = jnp.concatenate([meta_ref[...], jnp.zeros((120, TILE_C), F32)], axis=0).T
    return x_ref[...] + mt[:, 2:3] * ybuf[slot, 0] + mt[:, 3:4] * ybuf[slot, 1]


def _combine_kvq_body(pos_ref, y_hbm, x_ref, meta_ref, gkv_ref, wkv_ref, bkv_ref, gq_ref, wq_ref, bq_ref,
                      x2_ref, kv_ref, q_ref, ybuf, sem):
    x2 = _combined(pos_ref, y_hbm, x_ref, meta_ref, ybuf, sem)
    x2_ref[...] = x2
    kv_ref[...] = _dot(_rms(x2, gkv_ref[...]).astype(BF16), wkv_ref[...]) + bkv_ref[...]
    q_ref[...] = _dot(_rms(x2, gq_ref[...]).astype(BF16), wq_ref[...]) + bq_ref[...]


def _combine_kvq(pos, ys, x1, meta, gkv, wkv, bkv, gq, wq, bq):
    row = lambda i, p: (i, 0)
    consts = [gkv, wkv, bkv, gq, wq, bq]
    return pl.pallas_call(
        _combine_kvq_body,
        grid_spec=pltpu.PrefetchScalarGridSpec(
            num_scalar_prefetch=1,
            grid=(T // TILE_C,),
            in_specs=[pl.BlockSpec(memory_space=pl.ANY),
                      pl.BlockSpec((TILE_C, D), row),
                      pl.BlockSpec((8, TILE_C), lambda i, p: (0, i))]
                     + [pl.BlockSpec(c.shape, lambda i, p, nd=c.ndim: (0,) * nd) for c in consts],
            out_specs=[pl.BlockSpec((TILE_C, D), row),
                       pl.BlockSpec((TILE_C, 2 * KVW), row),
                       pl.BlockSpec((TILE_C, D), row)],
            scratch_shapes=[pltpu.VMEM((2, 2, TILE_C, D), F32), pltpu.SemaphoreType.DMA((2,))],
        ),
        out_shape=[jax.ShapeDtypeStruct((T, D), F32),
                   jax.ShapeDtypeStruct((T, 2 * KVW), F32),
                   jax.ShapeDtypeStruct((T, D), F32)],
        compiler_params=_cparams(),
        name="combine_kvq",
    )(pos, ys, x1, meta, *consts)


def _combine_final_body(pos_ref, y_hbm, x_ref, meta_ref, gf_ref, yp_ref, ys_ref, ybuf, sem):
    i = pl.program_id(0)
    out = _rms(_combined(pos_ref, y_hbm, x_ref, meta_ref, ybuf, sem), gf_ref[...])

    @pl.when(i < TP // TILE_C)
    def _():
        yp_ref[...] = out

    @pl.when(i >= TP // TILE_C)
    def _():
        ys_ref[...] = out


def _combine_final(pos, ys, x3, meta, gf):
    row = lambda i, p: (i, 0)
    npt = TP // TILE_C
    return pl.pallas_call(
        _combine_final_body,
        grid_spec=pltpu.PrefetchScalarGridSpec(
            num_scalar_prefetch=1,
            grid=(T // TILE_C,),
            in_specs=[pl.BlockSpec(memory_space=pl.ANY),
                      pl.BlockSpec((TILE_C, D), row),
                      pl.BlockSpec((8, TILE_C), lambda i, p: (0, i)),
                      pl.BlockSpec((1, D), lambda i, p: (0, 0))],
            out_specs=[pl.BlockSpec((TILE_C, D), lambda i, p: (jnp.minimum(i, npt - 1), 0)),
                       pl.BlockSpec((TILE_C, D), lambda i, p: (jnp.maximum(i - npt, 0), 0))],
            scratch_shapes=[pltpu.VMEM((2, 2, TILE_C, D), F32), pltpu.SemaphoreType.DMA((2,))],
        ),
        out_shape=[jax.ShapeDtypeStruct((TP, D), F32), jax.ShapeDtypeStruct((TS, D), F32)],
        compiler_params=_cparams(),
        name="combine_final",
    )(pos, ys, x3, meta, gf)


ATT_SUB = 2
TILE_A = ATT_SUB * WINDOW
NB_ATT = SEQ // TILE_A
NP_ATT = TP // TILE_A


def _attn_mixer_body(q_ref, kvc_ref, kvp_ref, x_ref, os_ref, bm_ref, band_ref, sink_ref,
                     wo_ref, bo_ref, gffn_ref, a_ref, rbias_ref, u_ref,
                     x3_ref, h_ref, meta_ref, cnt_out_ref, obuf, cnt_scr):
    i = pl.program_id(0)
    is_prompt = i < NP_ATT
    j = i % NB_ATT

    @pl.when(i == 0)
    def _():
        cnt_scr[...] = jnp.zeros_like(cnt_scr)

    @pl.when(is_prompt)
    def _():
        kall = jnp.concatenate([kvp_ref[:, :KVW], kvc_ref[:, :KVW]], axis=0).astype(BF16)
        vall = jnp.concatenate([kvp_ref[:, KVW:], kvc_ref[:, KVW:]], axis=0).astype(BF16)
        bm = bm_ref[...]
        bm16 = bm.astype(BF16)
        units = [(sb, r) for sb in range(ATT_SUB) for r in range(GQA)]
        kbs = [kall[sb * WINDOW:(sb + 2) * WINDOW] for sb in range(ATT_SUB)]
        vbs = [vall[sb * WINDOW:(sb + 2) * WINDOW] for sb in range(ATT_SUB)]
        qbs = [q_ref[sb * WINDOW:(sb + 1) * WINDOW, :].astype(BF16) for sb in range(ATT_SUB)]
        scores = []
        for sb, r in units:
            qm = jnp.concatenate([qbs[sb][:, r * KVW:(r + 1) * KVW]] * N_KV, axis=0) * bm16
            scores.append(_dot_nt(qm, kbs[sb]))
        valids = [band_ref[jnp.minimum(j, 1) if sb == 0 else 1] > 0.0 for sb in range(ATT_SUB)]
        probs, scales = [], []
        for (sb, r), s in zip(units, scores):
            s = jnp.where(valids[sb], s, MASK_VALUE)
            sink = sink_ref[r * N_KV * WINDOW:(r + 1) * N_KV * WINDOW]
            m = jnp.maximum(jnp.max(s, axis=-1, keepdims=True), sink)
            p = jnp.exp(s - m)
            den = jnp.sum(p, axis=-1, keepdims=True) + jnp.exp(sink - m)
            probs.append(p.astype(BF16))
            scales.append(1.0 / den)
        outs = [_dot(p, vbs[sb]) for (sb, r), p in zip(units, probs)]
        for (sb, r), o, sc in zip(units, outs, scales):
            o = o * (bm * sc)
            acc = o[0:WINDOW]
            for g in range(1, N_KV):
                acc = acc + o[g * WINDOW:(g + 1) * WINDOW]
            obuf[sb * WINDOW:(sb + 1) * WINDOW, r * KVW:(r + 1) * KVW] = acc

    @pl.when(jnp.logical_not(is_prompt))
    def _():
        obuf[...] = os_ref[...]

    x3 = x_ref[...] + _dot(obuf[...].astype(BF16), wo_ref[...]) + bo_ref[...]
    x3_ref[...] = x3
    h, meta = _moe_prologue(x3, gffn_ref[...], a_ref, rbias_ref, u_ref, cnt_scr)
    h_ref[...] = h
    meta_ref[...] = meta
    cnt_out_ref[...] = cnt_scr[...]


def _attn_mixer(q, kv, x2, o_s, bm, band, sink_col, wo, bo, gffn, a, rbias, u):
    consts = [bm, band, sink_col, wo, bo, gffn, a, rbias, u]
    prow = lambda i: (jnp.minimum(i, NP_ATT - 1), 0)

    def prev(i):
        ic = jnp.minimum(i, NP_ATT - 1)
        return (ATT_SUB * ic - jnp.where(ic % NB_ATT == 0, 0, 1), 0)

    row = lambda i: (i, 0)
    return pl.pallas_call(
        _attn_mixer_body,
        grid=(T // TILE_A,),
        in_specs=[pl.BlockSpec((TILE_A, D), prow),
                  pl.BlockSpec((TILE_A, 2 * KVW), prow),
                  pl.BlockSpec((WINDOW, 2 * KVW), prev),
                  pl.BlockSpec((TILE_A, D), row),
                  pl.BlockSpec((TILE_A, D), lambda i: (jnp.maximum(i - NP_ATT, 0), 0))]
                 + [_const_spec(c.shape) for c in consts],
        out_specs=[pl.BlockSpec((TILE_A, D), row),
                   pl.BlockSpec((TILE_A, D), row),
                   pl.BlockSpec((8, TILE_A), lambda i: (0, i)),
                   _const_spec((N_EXPERTS, 128))],
        out_shape=[jax.ShapeDtypeStruct((T, D), F32),
                   jax.ShapeDtypeStruct((T, D), F32),
                   jax.ShapeDtypeStruct((8, T), F32),
                   jax.ShapeDtypeStruct((N_EXPERTS, 128), F32)],
        scratch_shapes=[pltpu.VMEM((TILE_A, D), F32), pltpu.VMEM((N_EXPERTS, 128), F32)],
        compiler_params=_cparams(),
        name="attn_mixer",
    )(q, kv, kv, x2, o_s, *consts)


def _attn_sample_body(q_ref, kvn_ref, ck_ref, cv_ref, bm_ref, vc_ref, vn_ref, sink_ref,
                      o_ref, nk_ref, nv_ref):
    bm = bm_ref[...]
    valid_c = vc_ref[...] > 0.0
    valid_n = vn_ref[...] > 0.0
    sink = sink_ref[...]
    hist = WINDOW - DEC_SEQ

    def one(b, carry):
        q8 = q_ref[b]
        kvn = kvn_ref[b]
        qm = jnp.concatenate(
            [q8[:, r * KVW:(r + 1) * KVW] for r in range(GQA) for _ in range(N_KV)], axis=0)
        qm = (qm * bm).astype(BF16)
        s_c = jnp.where(valid_c, _dot_nt(qm, ck_ref[b].astype(BF16)), MASK_VALUE)
        s_n = jnp.where(valid_n, _dot_nt(qm, kvn[:, :KVW].astype(BF16)), MASK_VALUE)
        m = jnp.maximum(jnp.maximum(jnp.max(s_c, axis=-1, keepdims=True),
                                    jnp.max(s_n, axis=-1, keepdims=True)), sink)
        p_c = jnp.exp(s_c - m)
        p_n = jnp.exp(s_n - m)
        den = (jnp.sum(p_c, axis=-1, keepdims=True) + jnp.sum(p_n, axis=-1, keepdims=True)
               + jnp.exp(sink - m))
        o = (_dot(p_c.astype(BF16), cv_ref[b].astype(BF16))
             + _dot(p_n.astype(BF16), kvn[:, KVW:].astype(BF16))) * (bm * (1.0 / den))
        outs = []
        for r in range(GQA):
            acc = o[r * 32:r * 32 + 8]
            for g in range(1, N_KV):
                acc = acc + o[r * 32 + g * 8:r * 32 + g * 8 + 8]
            outs.append(acc)
        o_ref[b] = jnp.concatenate(outs, axis=1)
        nk_ref[b, 0:hist, :] = ck_ref[b, DEC_SEQ:WINDOW, :]
        nk_ref[b, hist:WINDOW, :] = kvn_ref[b, 0:DEC_SEQ, 0:KVW]
        nv_ref[b, 0:hist, :] = cv_ref[b, DEC_SEQ:WINDOW, :]
        nv_ref[b, hist:WINDOW, :] = kvn_ref[b, 0:DEC_SEQ, KVW:2 * KVW]
        return carry

    lax.fori_loop(0, SAMPLE_BB, one, 0, unroll=2)


def _attn_sample(q8, kvn16, ck, cv, bm, vc, vn, sink_col8):
    bb = SAMPLE_BB
    blk = lambda i: (i, 0, 0)
    consts = [bm, vc, vn, sink_col8]
    return pl.pallas_call(
        _attn_sample_body,
        grid=(DEC_BATCH // bb,),
        in_specs=[pl.BlockSpec((bb, 8, D), blk),
                  pl.BlockSpec((bb, NKN, 2 * KVW), blk),
                  pl.BlockSpec((bb, WINDOW, KVW), blk),
                  pl.BlockSpec((bb, WINDOW, KVW), blk)] + [_const_spec(c.shape) for c in consts],
        out_specs=[pl.BlockSpec((bb, 8, D), blk),
                   pl.BlockSpec((bb, WINDOW, KVW), blk),
                   pl.BlockSpec((bb, WINDOW, KVW), blk)],
        out_shape=[jax.ShapeDtypeStruct((DEC_BATCH, 8, D), F32),
                   jax.ShapeDtypeStruct((DEC_BATCH, WINDOW, KVW), F32),
                   jax.ShapeDtypeStruct((DEC_BATCH, WINDOW, KVW), F32)],
        compiler_params=_cparams(),
        name="attn_sample",
    )(q8, kvn16, ck, cv, *consts)


def _router_weights(w_group, b_group, w_router, b_router):
    wt = jnp.zeros((N_LOGIT_ROWS, D), F32)
    wt = wt.at[0:N_GROUPS].set(w_group.T)
    wt = wt.at[8:8 + N_EXPERTS].set(jnp.transpose(w_router, (0, 2, 1)).reshape(N_EXPERTS, D))
    hi = wt.astype(BF16)
    lo = (wt - hi.astype(F32)).astype(BF16)
    a = jnp.concatenate([hi, lo], axis=0)
    bias = jnp.zeros((N_LOGIT_ROWS,), F32)
    bias = bias.at[0:N_GROUPS].set(b_group)
    bias = bias.at[N_GROUPS:8].set(MASK_VALUE)
    bias = bias.at[8:8 + N_EXPERTS].set(b_router.reshape(N_EXPERTS))
    return a, jnp.broadcast_to(bias[:, None], (N_LOGIT_ROWS, 128))


def _routing_tables(cnt):
    counts = cnt[:, 0].astype(jnp.int32)
    padded = ((counts + TILE_M - 1) // TILE_M) * TILE_M
    ends = jnp.cumsum(padded)
    offs = ends - padded
    tile_start = jnp.arange(N_TILES, dtype=jnp.int32) * TILE_M
    n_used = (ends[-1] // TILE_M).astype(jnp.int32)
    tile_e = jnp.sum(tile_start[:, None] >= ends[None, :], axis=1).astype(jnp.int32)
    last_e = tile_e[jnp.maximum(n_used - 1, 0)]
    tile_e = jnp.where(tile_start < ends[-1], tile_e, last_e)
    zstart = jnp.concatenate([(offs + counts) // 8 * 8, ends[-1:]])
    zend = jnp.concatenate([ends, jnp.full((1,), P_ROWS, jnp.int32)])
    pad = jnp.concatenate([zstart, (zend - zstart) // 8]).astype(jnp.int32)
    return offs.astype(jnp.int32), pad, tile_e, n_used.reshape(1)


def _moe_layer(layer, h, meta, cnt, w1, w3, w2):
    offs, pad, tile_e, n_used = _routing_tables(cnt)
    pos = _positions(offs, meta).reshape(N_ASSIGN)
    xs = _scatter_rows(pos, pad, h)
    ys = _expert_ffn(layer, tile_e, n_used, xs, w1, w3, w2)
    return pos, ys


def _head_perm():
    idx = jnp.arange(D).reshape(N_KV, GQA, HEAD_DIM)
    return jnp.transpose(idx, (1, 0, 2)).reshape(D)


def _block_mask(rows_per_block, n_rep):
    n = n_rep * N_KV * rows_per_block
    r = (jnp.arange(n)[:, None] // rows_per_block) % N_KV
    c = jnp.arange(KVW)[None, :] // HEAD_DIM
    return (r == c).astype(F32)


def kernel(x_prompt, x_sample, state_conv, cache_k, cache_v, norm_mix, norm_ffn, conv_w_in, conv_b_in, conv_w_dw, conv_b_dw, conv_ln_g, conv_ln_b, conv_w_out, conv_b_out, norm_kv, w_kv, b_kv, attn_w_q, attn_b_q, attn_sinks, attn_w_o, attn_b_o, moe_w_group, moe_b_group, moe_w_router, moe_b_router, moe_w1, moe_w3, moe_w2, norm_final):
    r2 = lambda v: v.reshape(1, -1)
    u_tri = jnp.triu(jnp.ones((TILE_SEQ, TILE_SEQ), BF16), 1)
    ar0, rb0 = _router_weights(moe_w_group[0], moe_b_group[0], moe_w_router[0], moe_b_router[0])
    ar1, rb1 = _router_weights(moe_w_group[1], moe_b_group[1], moe_w_router[1], moe_b_router[1])

    win = conv_w_in[0].astype(BF16)
    wout = conv_w_out[0].astype(BF16)
    wdw = conv_w_dw[0]
    xs2 = x_sample.reshape(TS, D)
    u_s = _glu_sample(xs2, r2(norm_mix[0]), win, r2(conv_b_in[0]))
    tt = jnp.arange(DEC_SEQ)[:, None]
    jh = jnp.arange(CONV_HIST)[None, :]
    wh = jnp.where((jh >= tt)[..., None], wdw[jnp.clip(jh - tt, 0, CONV_W - 1)], 0.0)
    ju = jnp.arange(DEC_SEQ)[None, :]
    wu = jnp.where((ju <= tt)[..., None], wdw[jnp.clip(CONV_HIST - tt + ju, 0, CONV_W - 1)], 0.0)
    y_s, state_s = _dwconv_sample(u_s.reshape(DEC_BATCH, DEC_SEQ, D), state_conv, wh, wu)
    wdw8 = jnp.broadcast_to(wdw[:, None, :], (CONV_W, 8, D))
    x1, h, meta, cnt, state_p = _conv_mixer(
        x_prompt.reshape(TP, D), xs2, y_s.reshape(TS, D), r2(norm_mix[0]), win, r2(conv_b_in[0]), wdw8,
        r2(conv_b_dw[0]), r2(conv_ln_g[0]), r2(conv_ln_b[0]), wout, r2(conv_b_out[0]), r2(norm_ffn[0]),
        ar0, rb0, u_tri)

    pos, ys = _moe_layer(0, h, meta, cnt, moe_w1, moe_w3, moe_w2)
    perm = _head_perm()
    scale = HEAD_DIM ** -0.5
    wq = (attn_w_q[0][:, perm] * scale).astype(BF16)
    bq = r2(attn_b_q[0][perm] * scale)
    wo = attn_w_o[0][perm, :].astype(BF16)
    x2, kv, q = _combine_kvq(pos, ys, x1, meta, r2(norm_kv), w_kv.astype(BF16), r2(b_kv),
                             r2(norm_mix[1]), wq, bq)

    sinks = attn_sinks[0].astype(F32).reshape(N_KV, GQA).T
    sink_col = jnp.repeat(sinks.reshape(-1), WINDOW)[:, None]
    sink_col8 = jnp.repeat(sinks.reshape(-1), 8)[:, None]
    qi = (jnp.arange(GQA * N_KV * 8) % 8)[:, None]
    vc = ((qi < DEC_SEQ) & (jnp.arange(WINDOW)[None, :] > qi)).astype(F32)
    vn = ((qi < DEC_SEQ) & (jnp.arange(NKN)[None, :] <= qi)).astype(F32)
    q8 = jnp.pad(q[TP:].reshape(DEC_BATCH, DEC_SEQ, D), ((0, 0), (0, 8 - DEC_SEQ), (0, 0)))
    kvn = jnp.pad(kv[TP:].reshape(DEC_BATCH, DEC_SEQ, 2 * KVW), ((0, 0), (0, NKN - DEC_SEQ), (0, 0)))
    o_s8, nk_s, nv_s = _attn_sample(q8, kvn, cache_k.reshape(DEC_BATCH, WINDOW, KVW),
                                    cache_v.reshape(DEC_BATCH, WINDOW, KVW),
                                    _block_mask(8, GQA), vc, vn, sink_col8)
    o_s = o_s8[:, :DEC_SEQ].reshape(TS, D)
    pq = (jnp.arange(N_KV * WINDOW) % WINDOW)[:, None]
    pk = jnp.arange(2 * WINDOW)[None, :]
    dist = pq + WINDOW - pk
    band_rest = (dist >= 0) & (dist < WINDOW)
    band = jnp.stack([band_rest & (pk >= WINDOW), band_rest]).astype(F32)
    x3, h, meta, cnt = _attn_mixer(q, kv, x2, o_s, _block_mask(WINDOW, 1), band, sink_col, wo,
                                   r2(attn_b_o[0]), r2(norm_ffn[1]), ar1, rb1, u_tri[:TILE_A, :TILE_A])

    pos, ys = _moe_layer(1, h, meta, cnt, moe_w1, moe_w3, moe_w2)
    y_p, y_s2 = _combine_final(pos, ys, x3, meta, r2(norm_final))

    kvp = jnp.stack([kv[(b + 1) * SEQ - WINDOW:(b + 1) * SEQ] for b in range(BATCH)])
    new_k_p = kvp[..., :KVW].reshape(BATCH, WINDOW, N_KV, HEAD_DIM)
    new_v_p = kvp[..., KVW:].reshape(BATCH, WINDOW, N_KV, HEAD_DIM)
    return (y_p.reshape(BATCH, SEQ, D), y_s2.reshape(DEC_BATCH, DEC_SEQ, D),
            state_p[None], state_s, new_k_p, new_v_p,
            nk_s.reshape(DEC_BATCH, WINDOW, N_KV, HEAD_DIM), nv_s.reshape(DEC_BATCH, WINDOW, N_KV, HEAD_DIM))
```

```python
import jax
import jax.numpy as jnp
from jax import lax
from jax.experimental import pallas as pl
from jax.experimental.pallas import tpu as pltpu

D = 1024
BATCH = 8
SEQ = 2048
DEC_BATCH = 128
DEC_SEQ = 4
CONV_W = 31
CONV_HIST = CONV_W - 1
HEAD_DIM = 64
N_HEADS = 16
N_KV = 4
GQA = 4
KVW = N_KV * HEAD_DIM
WINDOW = 128
N_GROUPS = 4
EPG = 8
N_EXPERTS = 32
D_EXPERT = 512
RMS_EPS = 1e-5
LN_EPS = 1e-5
MASK_VALUE = -1e30

TP = BATCH * SEQ
TS = DEC_BATCH * DEC_SEQ
T = TP + TS
N_ASSIGN = 2 * T

TILE_SEQ = 512
TILE_M = 512
N_TILES = (N_ASSIGN + N_EXPERTS * (TILE_M - 1)) // TILE_M + 1
P_ROWS = N_TILES * TILE_M
TILE_C = 256
TILE_SC = 1536
N_LOGIT_ROWS = 48
SAMPLE_BB = 16
NKN = 16
ISSUE_UNROLL = 8
VMEM_LIMIT = 60 * 1024 * 1024

F32 = jnp.float32
BF16 = jnp.bfloat16


def _cparams(n_axes=1):
    return pltpu.CompilerParams(dimension_semantics=("arbitrary",) * n_axes,
                                vmem_limit_bytes=VMEM_LIMIT)


def _const_spec(shape):
    nd = len(shape)
    return pl.BlockSpec(shape, lambda *_: (0,) * nd)


def _rms(x, g):
    return x * lax.rsqrt(jnp.mean(x * x, axis=-1, keepdims=True) + RMS_EPS) * g


def _dot(a, b):
    return jnp.dot(a, b, preferred_element_type=F32)


def _dot_nt(a, b):
    return lax.dot_general(a, b, (((1,), (1,)), ((), ())), preferred_element_type=F32)


def _moe_prologue(x, g_ffn, a_ref, rbias_ref, u_ref, cnt_ref):
    n = x.shape[0]
    h = _rms(x, g_ffn)
    h_hi = h.astype(BF16)
    h_lo = (h - h_hi.astype(F32)).astype(BF16)
    a = a_ref[...]
    l1 = _dot_nt(a, h_hi)
    l2 = _dot_nt(a[:N_LOGIT_ROWS], h_lo)
    logits = l1[:N_LOGIT_ROWS] + l1[N_LOGIT_ROWS:] + l2 + rbias_ref[:, 0:1]
    iota8 = lax.broadcasted_iota(jnp.int32, (8, n), 0).astype(F32)
    gl = logits[0:8]
    gmax = jnp.max(gl, axis=0, keepdims=True)
    g_idx = jnp.min(jnp.where(gl == gmax, iota8, 8.0), axis=0, keepdims=True)
    g_w = 1.0 / jnp.sum(jnp.exp(gl - gmax), axis=0, keepdims=True)
    es = jnp.where(g_idx == 0.0, logits[8:16],
                   jnp.where(g_idx == 1.0, logits[16:24],
                             jnp.where(g_idx == 2.0, logits[24:32], logits[32:40])))
    v1 = jnp.max(es, axis=0, keepdims=True)
    i1 = jnp.min(jnp.where(es == v1, iota8, 8.0), axis=0, keepdims=True)
    es2 = jnp.where(iota8 == i1, -jnp.inf, es)
    v2 = jnp.max(es2, axis=0, keepdims=True)
    i2 = jnp.min(jnp.where(es2 == v2, iota8, 8.0), axis=0, keepdims=True)
    e2x = jnp.exp(v2 - v1)
    w1 = g_w / (1.0 + e2x)
    w2 = g_w * e2x / (1.0 + e2x)
    ex1 = g_idx * EPG + i1
    ex2 = g_idx * EPG + i2
    iota32 = lax.broadcasted_iota(jnp.int32, (N_EXPERTS, n), 0).astype(F32)
    oh1 = jnp.where(iota32 == ex1, 1.0, 0.0)
    oh2 = jnp.where(iota32 == ex2, 1.0, 0.0)
    onehot = oh1 + oh2
    before = _dot(onehot.astype(BF16), u_ref[...]) + cnt_ref[:, 0:1]
    rank1 = jnp.sum(oh1 * before, axis=0, keepdims=True)
    rank2 = jnp.sum(oh2 * before, axis=0, keepdims=True)
    cnt_ref[...] = cnt_ref[...] + jnp.sum(onehot, axis=1, keepdims=True)
    rows = (ex1, ex2, w1, w2, rank1, rank2)
    meta = jnp.zeros((8, n), F32)
    for r, val in enumerate(rows):
        meta = jnp.where(iota8 == r, val, meta)
    return h, meta


def _ln_silu(y, g, b):
    mu = jnp.mean(y, axis=-1, keepdims=True)
    yc = y - mu
    var = jnp.mean(yc * yc, axis=-1, keepdims=True)
    z = yc * lax.rsqrt(var + LN_EPS) * g + b
    return z * jax.nn.sigmoid(z)


CONV_CH = 32
NP_CONV = TP // TILE_SEQ
NSQ = SEQ // TILE_SEQ


def _conv_mixer_body(xp_ref, xs_ref, ys_ref, gmix_ref, win_ref, bin_ref, wdw_ref, bdw_ref,
                     lng_ref, lnb_ref, wout_ref, bout_ref, gffn_ref, a_ref, rbias_ref, u_ref,
                     x1_ref, h_ref, meta_ref, cnt_out_ref, state_ref,
                     ubuf, ush, ybuf, cnt_scr):
    i = pl.program_id(0)
    is_prompt = i < NP_CONV
    s = i % NSQ

    @pl.when(i == 0)
    def _():
        cnt_scr[...] = jnp.zeros_like(cnt_scr)

    @pl.when(is_prompt)
    def _():
        @pl.when(s == 0)
        def _():
            ubuf[0:32, :] = jnp.zeros((32, D), F32)

        hn = _rms(xp_ref[...], gmix_ref[...])
        u2 = _dot(hn.astype(BF16), win_ref[...]) + bin_ref[...]
        ubuf[32:32 + TILE_SEQ, :] = u2[:, :D] * jax.nn.sigmoid(u2[:, D:])

        for sh in range(1, 8):
            ush[sh - 1] = ubuf[sh:sh + TILE_SEQ + 24, :]

        ngrp = CONV_CH // 8

        def chunk(c, carry):
            r0 = pl.multiple_of(c * CONV_CH, CONV_CH)
            for lt in range(D // 128):
                lanes = slice(lt * 128, (lt + 1) * 128)
                acc = [None] * ngrp
                for sh in range(8):
                    a8s = [a8 for a8 in range(5) if 0 <= 8 * a8 + sh - 2 < CONV_W]
                    win = {}
                    for gi in range(a8s[0], a8s[-1] + ngrp):
                        start = pl.multiple_of(r0 + 8 * gi, 8)
                        if sh == 0:
                            win[gi] = ubuf[pl.ds(start, 8), lanes]
                        else:
                            win[gi] = ush[sh - 1, pl.ds(start, 8), lanes]
                    for a8 in a8s:
                        w = wdw_ref[8 * a8 + sh - 2, :, lanes]
                        for gq in range(ngrp):
                            term = w * win[a8 + gq]
                            acc[gq] = term if acc[gq] is None else acc[gq] + term
                for gq in range(ngrp):
                    ybuf[pl.ds(pl.multiple_of(r0 + 8 * gq, 8), 8), lanes] = acc[gq]
            return carry

        lax.fori_loop(0, TILE_SEQ // CONV_CH, chunk, 0)

        @pl.when(s == NSQ - 1)
        def _():
            state_ref[0] = ubuf[TILE_SEQ + 2:TILE_SEQ + 32, :]

        ubuf[0:32, :] = ubuf[TILE_SEQ:TILE_SEQ + 32, :]

    @pl.when(jnp.logical_not(is_prompt))
    def _():
        ybuf[...] = ys_ref[...]

    x = jnp.where(is_prompt, xp_ref[...], xs_ref[...])
    act = _ln_silu(ybuf[...] + bdw_ref[...], lng_ref[...], lnb_ref[...])
    x1 = x + _dot(act.astype(BF16), wout_ref[...]) + bout_ref[...]
    x1_ref[...] = x1
    h, meta = _moe_prologue(x1, gffn_ref[...], a_ref, rbias_ref, u_ref, cnt_scr)
    h_ref[...] = h
    meta_ref[...] = meta
    cnt_out_ref[...] = cnt_scr[...]


def _conv_mixer(xp, xs, ys, gmix, win, b_in, wdw, bdw, lng, lnb, wout, bout, gffn, a, rbias, u):
    consts = [gmix, win, b_in, wdw, bdw, lng, lnb, wout, bout, gffn, a, rbias, u]
    prow = lambda i: (jnp.minimum(i, NP_CONV - 1), 0)
    srow = lambda i: (jnp.maximum(i - NP_CONV, 0), 0)
    row = lambda i: (i, 0)
    return pl.pallas_call(
        _conv_mixer_body,
        grid=(T // TILE_SEQ,),
        in_specs=[pl.BlockSpec((TILE_SEQ, D), prow),
                  pl.BlockSpec((TILE_SEQ, D), srow),
                  pl.BlockSpec((TILE_SEQ, D), srow)] + [_const_spec(c.shape) for c in consts],
        out_specs=[
            pl.BlockSpec((TILE_SEQ, D), row),
            pl.BlockSpec((TILE_SEQ, D), row),
            pl.BlockSpec((8, TILE_SEQ), lambda i: (0, i)),
            _const_spec((N_EXPERTS, 128)),
            pl.BlockSpec((1, CONV_HIST, D), lambda i: (jnp.minimum(i // NSQ, BATCH - 1), 0, 0)),
        ],
        out_shape=[
            jax.ShapeDtypeStruct((T, D), F32),
            jax.ShapeDtypeStruct((T, D), F32),
            jax.ShapeDtypeStruct((8, T), F32),
            jax.ShapeDtypeStruct((N_EXPERTS, 128), F32),
            jax.ShapeDtypeStruct((BATCH, CONV_HIST, D), F32),
        ],
        scratch_shapes=[
            pltpu.VMEM((TILE_SEQ + 32, D), F32),
            pltpu.VMEM((7, TILE_SEQ + 24, D), F32),
            pltpu.VMEM((TILE_SEQ, D), F32),
            pltpu.VMEM((N_EXPERTS, 128), F32),
        ],
        compiler_params=_cparams(),
        name="conv_mixer",
    )(xp, xs, ys, *consts)


def _glu_sample_body(x_ref, gmix_ref, win_ref, bin_ref, u_ref):
    hn = _rms(x_ref[...], gmix_ref[...])
    u2 = _dot(hn.astype(BF16), win_ref[...]) + bin_ref[...]
    u_ref[...] = u2[:, :D] * jax.nn.sigmoid(u2[:, D:])


def _glu_sample(x, gmix, win, b_in):
    args = [x, gmix, win, b_in]
    return pl.pallas_call(
        _glu_sample_body,
        grid=(1,),
        in_specs=[_const_spec(a.shape) for a in args],
        out_specs=_const_spec((TS, D)),
        out_shape=jax.ShapeDtypeStruct((TS, D), F32),
        compiler_params=_cparams(),
        name="glu_sample",
    )(*args)


def _dwconv_sample_body(u_ref, st_ref, wh_ref, wu_ref, y_ref, nst_ref):
    st = st_ref[0]
    u = u_ref[...]
    for t in range(DEC_SEQ):
        y_ref[:, t:t + 1, :] = (jnp.sum(st * wh_ref[t][None], axis=1, keepdims=True)
                                + jnp.sum(u * wu_ref[t][None], axis=1, keepdims=True))
    nst_ref[0, :, 0:CONV_HIST - DEC_SEQ, :] = st_ref[0, :, DEC_SEQ:CONV_HIST, :]
    nst_ref[0, :, CONV_HIST - DEC_SEQ:CONV_HIST, :] = u


def _dwconv_sample(u3, state, wh, wu):
    bb = SAMPLE_BB
    blk = lambda i: (i, 0, 0)
    sblk = lambda i: (0, i, 0, 0)
    return pl.pallas_call(
        _dwconv_sample_body,
        grid=(DEC_BATCH // bb,),
        in_specs=[
            pl.BlockSpec((bb, DEC_SEQ, D), blk),
            pl.BlockSpec((1, bb, CONV_HIST, D), sblk),
            _const_spec(wh.shape), _const_spec(wu.shape),
        ],
        out_specs=[pl.BlockSpec((bb, DEC_SEQ, D), blk), pl.BlockSpec((1, bb, CONV_HIST, D), sblk)],
        out_shape=[
            jax.ShapeDtypeStruct((DEC_BATCH, DEC_SEQ, D), F32),
            jax.ShapeDtypeStruct((1, DEC_BATCH, CONV_HIST, D), F32),
        ],
        compiler_params=_cparams(),
        name="dwconv_sample",
    )(u3, state, wh, wu)


def _positions_body(offs_ref, meta_ref, pos_ref):
    ex = meta_ref[0:2, :]
    acc = meta_ref[4:6, :]
    for e in range(N_EXPERTS):
        acc = acc + jnp.where(ex == float(e), offs_ref[e].astype(F32), 0.0)
    pos_ref[...] = acc.astype(jnp.int32)


def _positions(offs, meta):
    return pl.pallas_call(
        _positions_body,
        grid_spec=pltpu.PrefetchScalarGridSpec(
            num_scalar_prefetch=1,
            grid=(1,),
            in_specs=[pl.BlockSpec((8, T), lambda i, o: (0, 0))],
            out_specs=pl.BlockSpec((2, T), lambda i, o: (0, 0)),
        ),
        out_shape=jax.ShapeDtypeStruct((2, T), jnp.int32),
        compiler_params=_cparams(),
        name="positions",
    )(offs, meta)


def _scatter_body(pos_ref, pad_ref, h_ref, xs_hbm, zbuf, sem):
    i = pl.program_id(0)

    @pl.when(i == 0)
    def _():
        zbuf[...] = jnp.zeros_like(zbuf)

        def per_region(e, total):
            zstart = pad_ref[e]
            nchunk = pad_ref[N_EXPERTS + 1 + e]

            def zissue(r, carry):
                dst = xs_hbm.at[pl.ds(pl.multiple_of(zstart + 8 * r, 8), 8), :]
                pltpu.make_async_copy(zbuf, dst, sem).start()
                return carry

            lax.fori_loop(0, nchunk, zissue, 0)
            return total + nchunk

        total = lax.fori_loop(0, N_EXPERTS + 1, per_region, jnp.int32(0))

        @pl.when(total > 0)
        def _():
            nrows = pl.multiple_of(total * 8, 8)
            pltpu.make_async_copy(xs_hbm.at[pl.ds(0, nrows), :],
                                  xs_hbm.at[pl.ds(0, nrows), :], sem).wait()

    @pl.when(i > 0)
    def _():
        base = (i - 1) * TILE_SC

        def issue(c, carry):
            for j in range(ISSUE_UNROLL):
                t = c * ISSUE_UNROLL + j
                src = h_ref.at[pl.ds(t, 1), :]
                for k in range(2):
                    dst = xs_hbm.at[pl.ds(pos_ref[k * T + base + t], 1), :]
                    pltpu.make_async_copy(src, dst, sem).start(priority=k)
            return carry

        lax.fori_loop(0, TILE_SC // ISSUE_UNROLL, issue, 0)
        for _ in range(2):
            pltpu.make_async_copy(h_ref, xs_hbm.at[pl.ds(0, TILE_SC), :], sem).wait()


def _scatter_rows(pos, pad, h):
    return pl.pallas_call(
        _scatter_body,
        grid_spec=pltpu.PrefetchScalarGridSpec(
            num_scalar_prefetch=2,
            grid=(1 + T // TILE_SC,),
            in_specs=[pl.BlockSpec((TILE_SC, D), lambda i, p, q: (jnp.maximum(i - 1, 0), 0))],
            out_specs=pl.BlockSpec(memory_space=pl.ANY),
            scratch_shapes=[pltpu.VMEM((8, D), F32), pltpu.SemaphoreType.DMA(())],
        ),
        out_shape=jax.ShapeDtypeStruct((P_ROWS, D), F32),
        compiler_params=_cparams(),
        name="scatter_rows",
    )(pos, pad, h)


def _expert_body(te_ref, nu_ref, x_ref, w1_ref, w3_ref, w2_ref, y_ref, w1b, w3b, w2b):
    i = pl.program_id(0)
    live = i < nu_ref[0]
    prev = te_ref[jnp.maximum(i - 1, 0)]
    fresh = (i == 0) | (te_ref[i] != prev)

    @pl.when(live & fresh)
    def _():
        w1b[...] = w1_ref[0, 0].astype(BF16)
        w3b[...] = w3_ref[0, 0].astype(BF16)
        w2b[...] = w2_ref[0, 0].astype(BF16)

    @pl.when(live)
    def _():
        xb = x_ref[...].astype(BF16)
        a = _dot(xb, w1b[...])
        g = _dot(xb, w3b[...])
        act = a * jax.nn.sigmoid(a) * g
        y_ref[...] = _dot(act.astype(BF16), w2b[...])

    @pl.when(jnp.logical_not(live))
    def _():
        y_ref[...] = jnp.zeros_like(y_ref)


def _expert_ffn(layer, tile_e, n_used, xs, w1, w3, w2):
    xrow = lambda i, te, nu: (jnp.minimum(i, nu[0] - 1), 0)
    wsel = lambda i, te, nu: (layer, te[i], 0, 0)
    return pl.pallas_call(
        _expert_body,
        grid_spec=pltpu.PrefetchScalarGridSpec(
            num_scalar_prefetch=2,
            grid=(N_TILES,),
            in_specs=[
                pl.BlockSpec((TILE_M, D), xrow),
                pl.BlockSpec((1, 1, D, D_EXPERT), wsel),
                pl.BlockSpec((1, 1, D, D_EXPERT), wsel),
                pl.BlockSpec((1, 1, D_EXPERT, D), wsel),
            ],
            out_specs=pl.BlockSpec((TILE_M, D), lambda i, te, nu: (i, 0)),
            scratch_shapes=[
                pltpu.VMEM((D, D_EXPERT), BF16),
                pltpu.VMEM((D, D_EXPERT), BF16),
                pltpu.VMEM((D_EXPERT, D), BF16),
            ],
        ),
        out_shape=jax.ShapeDtypeStruct((P_ROWS, D), F32),
        compiler_params=_cparams(),
        name="expert_ffn",
    )(tile_e, n_used, xs, w1, w3, w2)


def _gather_issue(pos_ref, y_hbm, ybuf, sem, tile, slot):
    base = tile * TILE_C

    def issue(c, carry):
        for j in range(ISSUE_UNROLL):
            t = c * ISSUE_UNROLL + j
            for k in range(2):
                pltpu.make_async_copy(y_hbm.at[pl.ds(pos_ref[k * T + base + t], 1), :],
                                      ybuf.at[slot, k, pl.ds(t, 1), :], sem.at[slot]).start(priority=k)
        return carry

    lax.fori_loop(0, TILE_C // ISSUE_UNROLL, issue, 0)


def _gather_wait(y_hbm, ybuf, sem, slot):
    for k in range(2):
        pltpu.make_async_copy(y_hbm.at[pl.ds(0, TILE_C), :], ybuf.at[slot, k], sem.at[slot]).wait()


def _combined(pos_ref, y_hbm, x_ref, meta_ref, ybuf, sem):
    i = pl.program_id(0)
    n = pl.num_programs(0)
    slot = i % 2

    @pl.when(i == 0)
    def _():
        _gather_issue(pos_ref, y_hbm, ybuf, sem, 0, 0)

    @pl.when(i + 1 < n)
    def _():
        _gather_issue(pos_ref, y_hbm, ybuf, sem, i + 1, 1 - slot)

    _gather_wait(y_hbm, ybuf, sem, slot)
    mt = jnp.concatenate([meta_ref[...], jnp.zeros((120, TILE_C), F32)], axis=0).T
    return x_ref[...] + mt[:, 2:3] * ybuf[slot, 0] + mt[:, 3:4] * ybuf[slot, 1]


def _combine_kvq_body(pos_ref, y_hbm, x_ref, meta_ref, gkv_ref, wkv_ref, bkv_ref, gq_ref, wq_ref, bq_ref,
                      x2_ref, kv_ref, q_ref, ybuf, sem):
    x2 = _combined(pos_ref, y_hbm, x_ref, meta_ref, ybuf, sem)
    x2_ref[...] = x2
    kv_ref[...] = _dot(_rms(x2, gkv_ref[...]).astype(BF16), wkv_ref[...]) + bkv_ref[...]
    q_ref[...] = _dot(_rms(x2, gq_ref[...]).astype(BF16), wq_ref[...]) + bq_ref[...]


def _combine_kvq(pos, ys, x1, meta, gkv, wkv, bkv, gq, wq, bq):
    row = lambda i, p: (i, 0)
    consts = [gkv, wkv, bkv, gq, wq, bq]
    return pl.pallas_call(
        _combine_kvq_body,
        grid_spec=pltpu.PrefetchScalarGridSpec(
            num_scalar_prefetch=1,
            grid=(T // TILE_C,),
            in_specs=[pl.BlockSpec(memory_space=pl.ANY),
                      pl.BlockSpec((TILE_C, D), row),
                      pl.BlockSpec((8, TILE_C), lambda i, p: (0, i))]
                     + [pl.BlockSpec(c.shape, lambda i, p, nd=c.ndim: (0,) * nd) for c in consts],
            out_specs=[pl.BlockSpec((TILE_C, D), row),
                       pl.BlockSpec((TILE_C, 2 * KVW), row),
                       pl.BlockSpec((TILE_C, D), row)],
            scratch_shapes=[pltpu.VMEM((2, 2, TILE_C, D), F32), pltpu.SemaphoreType.DMA((2,))],
        ),
        out_shape=[jax.ShapeDtypeStruct((T, D), F32),
                   jax.ShapeDtypeStruct((T, 2 * KVW), F32),
                   jax.ShapeDtypeStruct((T, D), F32)],
        compiler_params=_cparams(),
        name="combine_kvq",
    )(pos, ys, x1, meta, *consts)


def _combine_final_body(pos_ref, y_hbm, x_ref, meta_ref, gf_ref, yp_ref, ys_ref, ybuf, sem):
    i = pl.program_id(0)
    out = _rms(_combined(pos_ref, y_hbm, x_ref, meta_ref, ybuf, sem), gf_ref[...])

    @pl.when(i < TP // TILE_C)
    def _():
        yp_ref[...] = out

    @pl.when(i >= TP // TILE_C)
    def _():
        ys_ref[...] = out


def _combine_final(pos, ys, x3, meta, gf):
    row = lambda i, p: (i, 0)
    npt = TP // TILE_C
    return pl.pallas_call(
        _combine_final_body,
        grid_spec=pltpu.PrefetchScalarGridSpec(
            num_scalar_prefetch=1,
            grid=(T // TILE_C,),
            in_specs=[pl.BlockSpec(memory_space=pl.ANY),
                      pl.BlockSpec((TILE_C, D), row),
                      pl.BlockSpec((8, TILE_C), lambda i, p: (0, i)),
                      pl.BlockSpec((1, D), lambda i, p: (0, 0))],
            out_specs=[pl.BlockSpec((TILE_C, D), lambda i, p: (jnp.minimum(i, npt - 1), 0)),
                       pl.BlockSpec((TILE_C, D), lambda i, p: (jnp.maximum(i - npt, 0), 0))],
            scratch_shapes=[pltpu.VMEM((2, 2, TILE_C, D), F32), pltpu.SemaphoreType.DMA((2,))],
        ),
        out_shape=[jax.ShapeDtypeStruct((TP, D), F32), jax.ShapeDtypeStruct((TS, D), F32)],
        compiler_params=_cparams(),
        name="combine_final",
    )(pos, ys, x3, meta, gf)


ATT_SUB = 4
ATT_GROUP = 8
TILE_A = ATT_SUB * WINDOW
NB_ATT = SEQ // TILE_A
NP_ATT = TP // TILE_A


def _attn_mixer_body(q_ref, kvc_ref, kvp_ref, x_ref, os_ref, bm_ref, band_ref, sink_ref,
                     wo_ref, bo_ref, gffn_ref, a_ref, rbias_ref, u_ref,
                     x3_ref, h_ref, meta_ref, cnt_out_ref, obuf, cnt_scr):
    i = pl.program_id(0)
    is_prompt = i < NP_ATT
    j = i % NB_ATT

    @pl.when(i == 0)
    def _():
        cnt_scr[...] = jnp.zeros_like(cnt_scr)

    @pl.when(is_prompt)
    def _():
        kall = jnp.concatenate([kvp_ref[:, :KVW], kvc_ref[:, :KVW]], axis=0).astype(BF16)
        vall = jnp.concatenate([kvp_ref[:, KVW:], kvc_ref[:, KVW:]], axis=0).astype(BF16)
        bm = bm_ref[...]
        bm16 = bm.astype(BF16)
        kbs = [kall[sb * WINDOW:(sb + 2) * WINDOW] for sb in range(ATT_SUB)]
        vbs = [vall[sb * WINDOW:(sb + 2) * WINDOW] for sb in range(ATT_SUB)]
        qbs = [q_ref[sb * WINDOW:(sb + 1) * WINDOW, :].astype(BF16) for sb in range(ATT_SUB)]
        valids = [band_ref[jnp.minimum(j, 1) if sb == 0 else 1] > 0.0 for sb in range(ATT_SUB)]
        all_units = [(sb, r) for sb in range(ATT_SUB) for r in range(GQA)]
        for u0 in range(0, len(all_units), ATT_GROUP):
            units = all_units[u0:u0 + ATT_GROUP]
            scores = []
            for sb, r in units:
                qm = jnp.concatenate([qbs[sb][:, r * KVW:(r + 1) * KVW]] * N_KV, axis=0) * bm16
                scores.append(_dot_nt(qm, kbs[sb]))
            probs, scales = [], []
            for (sb, r), s in zip(units, scores):
                s = jnp.where(valids[sb], s, MASK_VALUE)
                sink = sink_ref[r * N_KV * WINDOW:(r + 1) * N_KV * WINDOW]
                m = jnp.maximum(jnp.max(s, axis=-1, keepdims=True), sink)
                p = jnp.exp(s - m)
                den = jnp.sum(p, axis=-1, keepdims=True) + jnp.exp(sink - m)
                probs.append(p.astype(BF16))
                scales.append(1.0 / den)
            outs = [_dot(p, vbs[sb]) for (sb, r), p in zip(units, probs)]
            for (sb, r), o, sc in zip(units, outs, scales):
                o = o * (bm * sc)
                acc = o[0:WINDOW]
                for g in range(1, N_KV):
                    acc = acc + o[g * WINDOW:(g + 1) * WINDOW]
                obuf[sb * WINDOW:(sb + 1) * WINDOW, r * KVW:(r + 1) * KVW] = acc

    @pl.when(jnp.logical_not(is_prompt))
    def _():
        obuf[...] = os_ref[...]

    x3 = x_ref[...] + _dot(obuf[...].astype(BF16), wo_ref[...]) + bo_ref[...]
    x3_ref[...] = x3
    h, meta = _moe_prologue(x3, gffn_ref[...], a_ref, rbias_ref, u_ref, cnt_scr)
    h_ref[...] = h
    meta_ref[...] = meta
    cnt_out_ref[...] = cnt_scr[...]


def _attn_mixer(q, kv, x2, o_s, bm, band, sink_col, wo, bo, gffn, a, rbias, u):
    consts = [bm, band, sink_col, wo, bo, gffn, a, rbias, u]
    prow = lambda i: (jnp.minimum(i, NP_ATT - 1), 0)

    def prev(i):
        ic = jnp.minimum(i, NP_ATT - 1)
        return (ATT_SUB * ic - jnp.where(ic % NB_ATT == 0, 0, 1), 0)

    row = lambda i: (i, 0)
    return pl.pallas_call(
        _attn_mixer_body,
        grid=(T // TILE_A,),
        in_specs=[pl.BlockSpec((TILE_A, D), prow),
                  pl.BlockSpec((TILE_A, 2 * KVW), prow),
                  pl.BlockSpec((WINDOW, 2 * KVW), prev),
                  pl.BlockSpec((TILE_A, D), row),
                  pl.BlockSpec((TILE_A, D), lambda i: (jnp.maximum(i - NP_ATT, 0), 0))]
                 + [_const_spec(c.shape) for c in consts],
        out_specs=[pl.BlockSpec((TILE_A, D), row),
                   pl.BlockSpec((TILE_A, D), row),
                   pl.BlockSpec((8, TILE_A), lambda i: (0, i)),
                   _const_spec((N_EXPERTS, 128))],
        out_shape=[jax.ShapeDtypeStruct((T, D), F32),
                   jax.ShapeDtypeStruct((T, D), F32),
                   jax.ShapeDtypeStruct((8, T), F32),
                   jax.ShapeDtypeStruct((N_EXPERTS, 128), F32)],
        scratch_shapes=[pltpu.VMEM((TILE_A, D), F32), pltpu.VMEM((N_EXPERTS, 128), F32)],
        compiler_params=_cparams(),
        name="attn_mixer",
    )(q, kv, kv, x2, o_s, *consts)


def _attn_sample_body(q_ref, kvn_ref, ck_ref, cv_ref, bm_ref, vc_ref, vn_ref, sink_ref,
                      o_ref, nk_ref, nv_ref):
    bm = bm_ref[...]
    valid_c = vc_ref[...] > 0.0
    valid_n = vn_ref[...] > 0.0
    sink = sink_ref[...]
    hist = WINDOW - DEC_SEQ

    def one(b, carry):
        q8 = q_ref[b]
        kvn = kvn_ref[b]
        qm = jnp.concatenate(
            [q8[:, r * KVW:(r + 1) * KVW] for r in range(GQA) for _ in range(N_KV)], axis=0)
        qm = (qm * bm).astype(BF16)
        s_c = jnp.where(valid_c, _dot_nt(qm, ck_ref[b].astype(BF16)), MASK_VALUE)
        s_n = jnp.where(valid_n, _dot_nt(qm, kvn[:, :KVW].astype(BF16)), MASK_VALUE)
        m = jnp.maximum(jnp.maximum(jnp.max(s_c, axis=-1, keepdims=True),
                                    jnp.max(s_n, axis=-1, keepdims=True)), sink)
        p_c = jnp.exp(s_c - m)
        p_n = jnp.exp(s_n - m)
        den = (jnp.sum(p_c, axis=-1, keepdims=True) + jnp.sum(p_n, axis=-1, keepdims=True)
               + jnp.exp(sink - m))
        o = (_dot(p_c.astype(BF16), cv_ref[b].astype(BF16))
             + _dot(p_n.astype(BF16), kvn[:, KVW:].astype(BF16))) * (bm * (1.0 / den))
        outs = []
        for r in range(GQA):
            acc = o[r * 32:r * 32 + 8]
            for g in range(1, N_KV):
                acc = acc + o[r * 32 + g * 8:r * 32 + g * 8 + 8]
            outs.append(acc)
        o_ref[b] = jnp.concatenate(outs, axis=1)
        nk_ref[b, 0:hist, :] = ck_ref[b, DEC_SEQ:WINDOW, :]
        nk_ref[b, hist:WINDOW, :] = kvn_ref[b, 0:DEC_SEQ, 0:KVW]
        nv_ref[b, 0:hist, :] = cv_ref[b, DEC_SEQ:WINDOW, :]
        nv_ref[b, hist:WINDOW, :] = kvn_ref[b, 0:DEC_SEQ, KVW:2 * KVW]
        return carry

    lax.fori_loop(0, SAMPLE_BB, one, 0, unroll=2)


def _attn_sample(q8, kvn16, ck, cv, bm, vc, vn, sink_col8):
    bb = SAMPLE_BB
    blk = lambda i: (i, 0, 0)
    consts = [bm, vc, vn, sink_col8]
    return pl.pallas_call(
        _attn_sample_body,
        grid=(DEC_BATCH // bb,),
        in_specs=[pl.BlockSpec((bb, 8, D), blk),
                  pl.BlockSpec((bb, NKN, 2 * KVW), blk),
                  pl.BlockSpec((bb, WINDOW, KVW), blk),
                  pl.BlockSpec((bb, WINDOW, KVW), blk)] + [_const_spec(c.shape) for c in consts],
        out_specs=[pl.BlockSpec((bb, 8, D), blk),
                   pl.BlockSpec((bb, WINDOW, KVW), blk),
                   pl.BlockSpec((bb, WINDOW, KVW), blk)],
        out_shape=[jax.ShapeDtypeStruct((DEC_BATCH, 8, D), F32),
                   jax.ShapeDtypeStruct((DEC_BATCH, WINDOW, KVW), F32),
                   jax.ShapeDtypeStruct((DEC_BATCH, WINDOW, KVW), F32)],
        compiler_params=_cparams(),
        name="attn_sample",
    )(q8, kvn16, ck, cv, *consts)


def _router_weights(w_group, b_group, w_router, b_router):
    wt = jnp.zeros((N_LOGIT_ROWS, D), F32)
    wt = wt.at[0:N_GROUPS].set(w_group.T)
    wt = wt.at[8:8 + N_EXPERTS].set(jnp.transpose(w_router, (0, 2, 1)).reshape(N_EXPERTS, D))
    hi = wt.astype(BF16)
    lo = (wt - hi.astype(F32)).astype(BF16)
    a = jnp.concatenate([hi, lo], axis=0)
    bias = jnp.zeros((N_LOGIT_ROWS,), F32)
    bias = bias.at[0:N_GROUPS].set(b_group)
    bias = bias.at[N_GROUPS:8].set(MASK_VALUE)
    bias = bias.at[8:8 + N_EXPERTS].set(b_router.reshape(N_EXPERTS))
    return a, jnp.broadcast_to(bias[:, None], (N_LOGIT_ROWS, 128))


def _routing_tables(cnt):
    counts = cnt[:, 0].astype(jnp.int32)
    padded = ((counts + TILE_M - 1) // TILE_M) * TILE_M
    ends = jnp.cumsum(padded)
    offs = ends - padded
    tile_start = jnp.arange(N_TILES, dtype=jnp.int32) * TILE_M
    n_used = (ends[-1] // TILE_M).astype(jnp.int32)
    tile_e = jnp.sum(tile_start[:, None] >= ends[None, :], axis=1).astype(jnp.int32)
    last_e = tile_e[jnp.maximum(n_used - 1, 0)]
    tile_e = jnp.where(tile_start < ends[-1], tile_e, last_e)
    zstart = jnp.concatenate([(offs + counts) // 8 * 8, ends[-1:]])
    zend = jnp.concatenate([ends, jnp.full((1,), P_ROWS, jnp.int32)])
    pad = jnp.concatenate([zstart, (zend - zstart) // 8]).astype(jnp.int32)
    return offs.astype(jnp.int32), pad, tile_e, n_used.reshape(1)


def _moe_layer(layer, h, meta, cnt, w1, w3, w2):
    offs, pad, tile_e, n_used = _routing_tables(cnt)
    pos = _positions(offs, meta).reshape(N_ASSIGN)
    xs = _scatter_rows(pos, pad, h)
    ys = _expert_ffn(layer, tile_e, n_used, xs, w1, w3, w2)
    return pos, ys


def _head_perm():
    idx = jnp.arange(D).reshape(N_KV, GQA, HEAD_DIM)
    return jnp.transpose(idx, (1, 0, 2)).reshape(D)


def _block_mask(rows_per_block, n_rep):
    n = n_rep * N_KV * rows_per_block
    r = (jnp.arange(n)[:, None] // rows_per_block) % N_KV
    c = jnp.arange(KVW)[None, :] // HEAD_DIM
    return (r == c).astype(F32)


def kernel(x_prompt, x_sample, state_conv, cache_k, cache_v, norm_mix, norm_ffn, conv_w_in, conv_b_in, conv_w_dw, conv_b_dw, conv_ln_g, conv_ln_b, conv_w_out, conv_b_out, norm_kv, w_kv, b_kv, attn_w_q, attn_b_q, attn_sinks, attn_w_o, attn_b_o, moe_w_group, moe_b_group, moe_w_router, moe_b_router, moe_w1, moe_w3, moe_w2, norm_final):
    r2 = lambda v: v.reshape(1, -1)
    n_tri = max(TILE_SEQ, TILE_A)
    u_tri = jnp.triu(jnp.ones((n_tri, n_tri), BF16), 1)
    ar0, rb0 = _router_weights(moe_w_group[0], moe_b_group[0], moe_w_router[0], moe_b_router[0])
    ar1, rb1 = _router_weights(moe_w_group[1], moe_b_group[1], moe_w_router[1], moe_b_router[1])

    win = conv_w_in[0].astype(BF16)
    wout = conv_w_out[0].astype(BF16)
    wdw = conv_w_dw[0]
    xs2 = x_sample.reshape(TS, D)
    u_s = _glu_sample(xs2, r2(norm_mix[0]), win, r2(conv_b_in[0]))
    tt = jnp.arange(DEC_SEQ)[:, None]
    jh = jnp.arange(CONV_HIST)[None, :]
    wh = jnp.where((jh >= tt)[..., None], wdw[jnp.clip(jh - tt, 0, CONV_W - 1)], 0.0)
    ju = jnp.arange(DEC_SEQ)[None, :]
    wu = jnp.where((ju <= tt)[..., None], wdw[jnp.clip(CONV_HIST - tt + ju, 0, CONV_W - 1)], 0.0)
    y_s, state_s = _dwconv_sample(u_s.reshape(DEC_BATCH, DEC_SEQ, D), state_conv, wh, wu)
    wdw8 = jnp.broadcast_to(wdw[:, None, :], (CONV_W, 8, D))
    x1, h, meta, cnt, state_p = _conv_mixer(
        x_prompt.reshape(TP, D), xs2, y_s.reshape(TS, D), r2(norm_mix[0]), win, r2(conv_b_in[0]), wdw8,
        r2(conv_b_dw[0]), r2(conv_ln_g[0]), r2(conv_ln_b[0]), wout, r2(conv_b_out[0]), r2(norm_ffn[0]),
        ar0, rb0, u_tri[:TILE_SEQ, :TILE_SEQ])

    pos, ys = _moe_layer(0, h, meta, cnt, moe_w1, moe_w3, moe_w2)
    perm = _head_perm()
    scale = HEAD_DIM ** -0.5
    wq = (attn_w_q[0][:, perm] * scale).astype(BF16)
    bq = r2(attn_b_q[0][perm] * scale)
    wo = attn_w_o[0][perm, :].astype(BF16)
    x2, kv, q = _combine_kvq(pos, ys, x1, meta, r2(norm_kv), w_kv.astype(BF16), r2(b_kv),
                             r2(norm_mix[1]), wq, bq)

    sinks = attn_sinks[0].astype(F32).reshape(N_KV, GQA).T
    sink_col = jnp.repeat(sinks.reshape(-1), WINDOW)[:, None]
    sink_col8 = jnp.repeat(sinks.reshape(-1), 8)[:, None]
    qi = (jnp.arange(GQA * N_KV * 8) % 8)[:, None]
    vc = ((qi < DEC_SEQ) & (jnp.arange(WINDOW)[None, :] > qi)).astype(F32)
    vn = ((qi < DEC_SEQ) & (jnp.arange(NKN)[None, :] <= qi)).astype(F32)
    q8 = jnp.pad(q[TP:].reshape(DEC_BATCH, DEC_SEQ, D), ((0, 0), (0, 8 - DEC_SEQ), (0, 0)))
    kvn = jnp.pad(kv[TP:].reshape(DEC_BATCH, DEC_SEQ, 2 * KVW), ((0, 0), (0, NKN - DEC_SEQ), (0, 0)))
    o_s8, nk_s, nv_s = _attn_sample(q8, kvn, cache_k.reshape(DEC_BATCH, WINDOW, KVW),
                                    cache_v.reshape(DEC_BATCH, WINDOW, KVW),
                                    _block_mask(8, GQA), vc, vn, sink_col8)
    o_s = o_s8[:, :DEC_SEQ].reshape(TS, D)
    pq = (jnp.arange(N_KV * WINDOW) % WINDOW)[:, None]
    pk = jnp.arange(2 * WINDOW)[None, :]
    dist = pq + WINDOW - pk
    band_rest = (dist >= 0) & (dist < WINDOW)
    band = jnp.stack([band_rest & (pk >= WINDOW), band_rest]).astype(F32)
    x3, h, meta, cnt = _attn_mixer(q, kv, x2, o_s, _block_mask(WINDOW, 1), band, sink_col, wo,
                                   r2(attn_b_o[0]), r2(norm_ffn[1]), ar1, rb1, u_tri[:TILE_A, :TILE_A])

    pos, ys = _moe_layer(1, h, meta, cnt, moe_w1, moe_w3, moe_w2)
    y_p, y_s2 = _combine_final(pos, ys, x3, meta, r2(norm_final))

    kvp = jnp.stack([kv[(b + 1) * SEQ - WINDOW:(b + 1) * SEQ] for b in range(BATCH)])
    new_k_p = kvp[..., :KVW].reshape(BATCH, WINDOW, N_KV, HEAD_DIM)
    new_v_p = kvp[..., KVW:].reshape(BATCH, WINDOW, N_KV, HEAD_DIM)
    return (y_p.reshape(BATCH, SEQ, D), y_s2.reshape(DEC_BATCH, DEC_SEQ, D),
            state_p[None], state_s, new_k_p, new_v_p,
            nk_s.reshape(DEC_BATCH, WINDOW, N_KV, HEAD_DIM), nv_s.reshape(DEC_BATCH, WINDOW, N_KV, HEAD_DIM))
```

```python
import jax
import jax.numpy as jnp
from jax import lax
from jax.experimental import pallas as pl
from jax.experimental.pallas import tpu as pltpu

D = 1024
BATCH = 8
SEQ = 2048
DEC_BATCH = 128
DEC_SEQ = 4
CONV_W = 31
CONV_HIST = CONV_W - 1
HEAD_DIM = 64
N_HEADS = 16
N_KV = 4
GQA = 4
KVW = N_KV * HEAD_DIM
WINDOW = 128
N_GROUPS = 4
EPG = 8
N_EXPERTS = 32
D_EXPERT = 512
RMS_EPS = 1e-5
LN_EPS = 1e-5
MASK_VALUE = -1e30

TP = BATCH * SEQ
TS = DEC_BATCH * DEC_SEQ
T = TP + TS
N_ASSIGN = 2 * T

TILE_SEQ = 512
TILE_M = 512
N_TILES = (N_ASSIGN + N_EXPERTS * (TILE_M - 1)) // TILE_M + 1
P_ROWS = N_TILES * TILE_M
TILE_C = 256
TILE_SC = 1536
N_LOGIT_ROWS = 48
SAMPLE_BB = 16
NKN = 16
ISSUE_UNROLL = 8
VMEM_LIMIT = 60 * 1024 * 1024

F32 = jnp.float32
BF16 = jnp.bfloat16


def _cparams(n_axes=1):
    return pltpu.CompilerParams(dimension_semantics=("arbitrary",) * n_axes,
                                vmem_limit_bytes=VMEM_LIMIT)


def _const_spec(shape):
    nd = len(shape)
    return pl.BlockSpec(shape, lambda *_: (0,) * nd)


def _rms(x, g):
    return x * lax.rsqrt(jnp.mean(x * x, axis=-1, keepdims=True) + RMS_EPS) * g


def _dot(a, b):
    return jnp.dot(a, b, preferred_element_type=F32)


def _dot_nt(a, b):
    return lax.dot_general(a, b, (((1,), (1,)), ((), ())), preferred_element_type=F32)


def _moe_prologue(x, g_ffn, a_ref, rbias_ref, u_ref, cnt_ref):
    n = x.shape[0]
    h = _rms(x, g_ffn)
    h_hi = h.astype(BF16)
    h_lo = (h - h_hi.astype(F32)).astype(BF16)
    a = a_ref[...]
    l1 = _dot_nt(a, h_hi)
    l2 = _dot_nt(a[:N_LOGIT_ROWS], h_lo)
    logits = l1[:N_LOGIT_ROWS] + l1[N_LOGIT_ROWS:] + l2 + rbias_ref[:, 0:1]
    iota8 = lax.broadcasted_iota(jnp.int32, (8, n), 0).astype(F32)
    gl = logits[0:8]
    gmax = jnp.max(gl, axis=0, keepdims=True)
    g_idx = jnp.min(jnp.where(gl == gmax, iota8, 8.0), axis=0, keepdims=True)
    g_w = 1.0 / jnp.sum(jnp.exp(gl - gmax), axis=0, keepdims=True)
    es = jnp.where(g_idx == 0.0, logits[8:16],
                   jnp.where(g_idx == 1.0, logits[16:24],
                             jnp.where(g_idx == 2.0, logits[24:32], logits[32:40])))
    v1 = jnp.max(es, axis=0, keepdims=True)
    i1 = jnp.min(jnp.where(es == v1, iota8, 8.0), axis=0, keepdims=True)
    es2 = jnp.where(iota8 == i1, -jnp.inf, es)
    v2 = jnp.max(es2, axis=0, keepdims=True)
    i2 = jnp.min(jnp.where(es2 == v2, iota8, 8.0), axis=0, keepdims=True)
    e2x = jnp.exp(v2 - v1)
    w1 = g_w / (1.0 + e2x)
    w2 = g_w * e2x / (1.0 + e2x)
    ex1 = g_idx * EPG + i1
    ex2 = g_idx * EPG + i2
    iota32 = lax.broadcasted_iota(jnp.int32, (N_EXPERTS, n), 0).astype(F32)
    oh1 = jnp.where(iota32 == ex1, 1.0, 0.0)
    oh2 = jnp.where(iota32 == ex2, 1.0, 0.0)
    onehot = oh1 + oh2
    before = _dot(onehot.astype(BF16), u_ref[...]) + cnt_ref[:, 0:1]
    rank1 = jnp.sum(oh1 * before, axis=0, keepdims=True)
    rank2 = jnp.sum(oh2 * before, axis=0, keepdims=True)
    cnt_ref[...] = cnt_ref[...] + jnp.sum(onehot, axis=1, keepdims=True)
    rows = (ex1, ex2, w1, w2, rank1, rank2)
    meta = jnp.zeros((8, n), F32)
    for r, val in enumerate(rows):
        meta = jnp.where(iota8 == r, val, meta)
    return h, meta


def _ln_silu(y, g, b):
    mu = jnp.mean(y, axis=-1, keepdims=True)
    yc = y - mu
    var = jnp.mean(yc * yc, axis=-1, keepdims=True)
    z = yc * lax.rsqrt(var + LN_EPS) * g + b
    return z * jax.nn.sigmoid(z)


CONV_CH = 32
NP_CONV = TP // TILE_SEQ
NSQ = SEQ // TILE_SEQ


def _conv_mixer_body(xp_ref, xs_ref, ys_ref, gmix_ref, win_ref, bin_ref, wdw_ref, bdw_ref,
                     lng_ref, lnb_ref, wout_ref, bout_ref, gffn_ref, a_ref, rbias_ref, u_ref,
                     x1_ref, h_ref, meta_ref, cnt_out_ref, state_ref,
                     ubuf, ush, ybuf, cnt_scr):
    i = pl.program_id(0)
    is_prompt = i < NP_CONV
    s = i % NSQ

    @pl.when(i == 0)
    def _():
        cnt_scr[...] = jnp.zeros_like(cnt_scr)

    @pl.when(is_prompt)
    def _():
        @pl.when(s == 0)
        def _():
            ubuf[0:32, :] = jnp.zeros((32, D), F32)

        hn = _rms(xp_ref[...], gmix_ref[...])
        u2 = _dot(hn.astype(BF16), win_ref[...]) + bin_ref[...]
        ubuf[32:32 + TILE_SEQ, :] = u2[:, :D] * jax.nn.sigmoid(u2[:, D:])

        for sh in range(1, 8):
            ush[sh - 1] = ubuf[sh:sh + TILE_SEQ + 24, :]

        ngrp = CONV_CH // 8

        def chunk(c, carry):
            r0 = pl.multiple_of(c * CONV_CH, CONV_CH)
            for lt in range(D // 128):
                lanes = slice(lt * 128, (lt + 1) * 128)
                acc = [None] * ngrp
                for sh in range(8):
                    a8s = [a8 for a8 in range(5) if 0 <= 8 * a8 + sh - 2 < CONV_W]
                    win = {}
                    for gi in range(a8s[0], a8s[-1] + ngrp):
                        start = pl.multiple_of(r0 + 8 * gi, 8)
                        if sh == 0:
                            win[gi] = ubuf[pl.ds(start, 8), lanes]
                        else:
                            win[gi] = ush[sh - 1, pl.ds(start, 8), lanes]
                    for a8 in a8s:
                        w = wdw_ref[8 * a8 + sh - 2, :, lanes]
                        for gq in range(ngrp):
                            term = w * win[a8 + gq]
                            acc[gq] = term if acc[gq] is None else acc[gq] + term
                for gq in range(ngrp):
                    ybuf[pl.ds(pl.multiple_of(r0 + 8 * gq, 8), 8), lanes] = acc[gq]
            return carry

        lax.fori_loop(0, TILE_SEQ // CONV_CH, chunk, 0)

        @pl.when(s == NSQ - 1)
        def _():
            state_ref[0] = ubuf[TILE_SEQ + 2:TILE_SEQ + 32, :]

        ubuf[0:32, :] = ubuf[TILE_SEQ:TILE_SEQ + 32, :]

    @pl.when(jnp.logical_not(is_prompt))
    def _():
        ybuf[...] = ys_ref[...]

    x = jnp.where(is_prompt, xp_ref[...], xs_ref[...])
    act = _ln_silu(ybuf[...] + bdw_ref[...], lng_ref[...], lnb_ref[...])
    x1 = x + _dot(act.astype(BF16), wout_ref[...]) + bout_ref[...]
    x1_ref[...] = x1
    h, meta = _moe_prologue(x1, gffn_ref[...], a_ref, rbias_ref, u_ref, cnt_scr)
    h_ref[...] = h
    meta_ref[...] = meta
    cnt_out_ref[...] = cnt_scr[...]


def _conv_mixer(xp, xs, ys, gmix, win, b_in, wdw, bdw, lng, lnb, wout, bout, gffn, a, rbias, u):
    consts = [gmix, win, b_in, wdw, bdw, lng, lnb, wout, bout, gffn, a, rbias, u]
    prow = lambda i: (jnp.minimum(i, NP_CONV - 1), 0)
    srow = lambda i: (jnp.maximum(i - NP_CONV, 0), 0)
    row = lambda i: (i, 0)
    return pl.pallas_call(
        _conv_mixer_body,
        grid=(T // TILE_SEQ,),
        in_specs=[pl.BlockSpec((TILE_SEQ, D), prow),
                  pl.BlockSpec((TILE_SEQ, D), srow),
                  pl.BlockSpec((TILE_SEQ, D), srow)] + [_const_spec(c.shape) for c in consts],
        out_specs=[
            pl.BlockSpec((TILE_SEQ, D), row),
            pl.BlockSpec((TILE_SEQ, D), row),
            pl.BlockSpec((8, TILE_SEQ), lambda i: (0, i)),
            _const_spec((N_EXPERTS, 128)),
            pl.BlockSpec((1, CONV_HIST, D), lambda i: (jnp.minimum(i // NSQ, BATCH - 1), 0, 0)),
        ],
        out_shape=[
            jax.ShapeDtypeStruct((T, D), F32),
            jax.ShapeDtypeStruct((T, D), F32),
            jax.ShapeDtypeStruct((8, T), F32),
            jax.ShapeDtypeStruct((N_EXPERTS, 128), F32),
            jax.ShapeDtypeStruct((BATCH, CONV_HIST, D), F32),
        ],
        scratch_shapes=[
            pltpu.VMEM((TILE_SEQ + 32, D), F32),
            pltpu.VMEM((7, TILE_SEQ + 24, D), F32),
            pltpu.VMEM((TILE_SEQ, D), F32),
            pltpu.VMEM((N_EXPERTS, 128), F32),
        ],
        compiler_params=_cparams(),
        name="conv_mixer",
    )(xp, xs, ys, *consts)


def _glu_sample_body(x_ref, gmix_ref, win_ref, bin_ref, u_ref):
    hn = _rms(x_ref[...], gmix_ref[...])
    u2 = _dot(hn.astype(BF16), win_ref[...]) + bin_ref[...]
    u_ref[...] = u2[:, :D] * jax.nn.sigmoid(u2[:, D:])


def _glu_sample(x, gmix, win, b_in):
    args = [x, gmix, win, b_in]
    return pl.pallas_call(
        _glu_sample_body,
        grid=(1,),
        in_specs=[_const_spec(a.shape) for a in args],
        out_specs=_const_spec((TS, D)),
        out_shape=jax.ShapeDtypeStruct((TS, D), F32),
        compiler_params=_cparams(),
        name="glu_sample",
    )(*args)


def _dwconv_sample_body(u_ref, st_ref, wh_ref, wu_ref, y_ref, nst_ref):
    st = st_ref[0]
    u = u_ref[...]
    for t in range(DEC_SEQ):
        y_ref[:, t:t + 1, :] = (jnp.sum(st * wh_ref[t][None], axis=1, keepdims=True)
                                + jnp.sum(u * wu_ref[t][None], axis=1, keepdims=True))
    nst_ref[0, :, 0:CONV_HIST - DEC_SEQ, :] = st_ref[0, :, DEC_SEQ:CONV_HIST, :]
    nst_ref[0, :, CONV_HIST - DEC_SEQ:CONV_HIST, :] = u


def _dwconv_sample(u3, state, wh, wu):
    bb = SAMPLE_BB
    blk = lambda i: (i, 0, 0)
    sblk = lambda i: (0, i, 0, 0)
    return pl.pallas_call(
        _dwconv_sample_body,
        grid=(DEC_BATCH // bb,),
        in_specs=[
            pl.BlockSpec((bb, DEC_SEQ, D), blk),
            pl.BlockSpec((1, bb, CONV_HIST, D), sblk),
            _const_spec(wh.shape), _const_spec(wu.shape),
        ],
        out_specs=[pl.BlockSpec((bb, DEC_SEQ, D), blk), pl.BlockSpec((1, bb, CONV_HIST, D), sblk)],
        out_shape=[
            jax.ShapeDtypeStruct((DEC_BATCH, DEC_SEQ, D), F32),
            jax.ShapeDtypeStruct((1, DEC_BATCH, CONV_HIST, D), F32),
        ],
        compiler_params=_cparams(),
        name="dwconv_sample",
    )(u3, state, wh, wu)


def _positions_body(offs_ref, meta_ref, pos_ref):
    ex = meta_ref[0:2, :]
    acc = meta_ref[4:6, :]
    for e in range(N_EXPERTS):
        acc = acc + jnp.where(ex == float(e), offs_ref[e].astype(F32), 0.0)
    pos_ref[...] = acc.astype(jnp.int32)


def _positions(offs, meta):
    return pl.pallas_call(
        _positions_body,
        grid_spec=pltpu.PrefetchScalarGridSpec(
            num_scalar_prefetch=1,
            grid=(1,),
            in_specs=[pl.BlockSpec((8, T), lambda i, o: (0, 0))],
            out_specs=pl.BlockSpec((2, T), lambda i, o: (0, 0)),
        ),
        out_shape=jax.ShapeDtypeStruct((2, T), jnp.int32),
        compiler_params=_cparams(),
        name="positions",
    )(offs, meta)


def _scatter_body(pos_ref, pad_ref, h_ref, xs_hbm, zbuf, sem):
    i = pl.program_id(0)

    @pl.when(i == 0)
    def _():
        zbuf[...] = jnp.zeros_like(zbuf)

        def per_region(e, total):
            zstart = pad_ref[e]
            nchunk = pad_ref[N_EXPERTS + 1 + e]

            def zissue(r, carry):
                dst = xs_hbm.at[pl.ds(pl.multiple_of(zstart + 8 * r, 8), 8), :]
                pltpu.make_async_copy(zbuf, dst, sem).start()
                return carry

            lax.fori_loop(0, nchunk, zissue, 0)
            return total + nchunk

        total = lax.fori_loop(0, N_EXPERTS + 1, per_region, jnp.int32(0))

        @pl.when(total > 0)
        def _():
            nrows = pl.multiple_of(total * 8, 8)
            pltpu.make_async_copy(xs_hbm.at[pl.ds(0, nrows), :],
                                  xs_hbm.at[pl.ds(0, nrows), :], sem).wait()

    @pl.when(i > 0)
    def _():
        base = (i - 1) * TILE_SC

        def issue(c, carry):
            for j in range(8):
                t = c * 8 + j
                src = h_ref.at[c, pl.ds(j, 1), :]
                for k in range(2):
                    dst = xs_hbm.at[pl.ds(pos_ref[k * T + base + t], 1), :]
                    pltpu.make_async_copy(src, dst, sem).start(priority=k)
            return carry

        lax.fori_loop(0, TILE_SC // 8, issue, 0)
        for _ in range(2):
            pltpu.make_async_copy(xs_hbm.at[pl.ds(0, TILE_SC), :], xs_hbm.at[pl.ds(0, TILE_SC), :], sem).wait()


def _scatter_rows(pos, pad, h):
    return pl.pallas_call(
        _scatter_body,
        grid_spec=pltpu.PrefetchScalarGridSpec(
            num_scalar_prefetch=2,
            grid=(1 + T // TILE_SC,),
            in_specs=[pl.BlockSpec((TILE_SC // 8, 8, D), lambda i, p, q: (jnp.maximum(i - 1, 0), 0, 0))],
            out_specs=pl.BlockSpec(memory_space=pl.ANY),
            scratch_shapes=[pltpu.VMEM((8, D), F32), pltpu.SemaphoreType.DMA(())],
        ),
        out_shape=jax.ShapeDtypeStruct((P_ROWS, D), F32),
        compiler_params=_cparams(),
        name="scatter_rows",
    )(pos, pad, h.reshape(T // 8, 8, D))


def _expert_body(te_ref, nu_ref, x_ref, w1_ref, w3_ref, w2_ref, y_ref, w1b, w3b, w2b):
    i = pl.program_id(0)
    live = i < nu_ref[0]
    prev = te_ref[jnp.maximum(i - 1, 0)]
    fresh = (i == 0) | (te_ref[i] != prev)

    @pl.when(live & fresh)
    def _():
        w1b[...] = w1_ref[0, 0].astype(BF16)
        w3b[...] = w3_ref[0, 0].astype(BF16)
        w2b[...] = w2_ref[0, 0].astype(BF16)

    @pl.when(live)
    def _():
        xb = x_ref[...].astype(BF16)
        a = _dot(xb, w1b[...])
        g = _dot(xb, w3b[...])
        act = a * jax.nn.sigmoid(a) * g
        y_ref[...] = _dot(act.astype(BF16), w2b[...])

    @pl.when(jnp.logical_not(live))
    def _():
        y_ref[...] = jnp.zeros_like(y_ref)


def _expert_ffn(layer, tile_e, n_used, xs, w1, w3, w2):
    xrow = lambda i, te, nu: (jnp.minimum(i, nu[0] - 1), 0)
    wsel = lambda i, te, nu: (layer, te[i], 0, 0)
    return pl.pallas_call(
        _expert_body,
        grid_spec=pltpu.PrefetchScalarGridSpec(
            num_scalar_prefetch=2,
            grid=(N_TILES,),
            in_specs=[
                pl.BlockSpec((TILE_M, D), xrow),
                pl.BlockSpec((1, 1, D, D_EXPERT), wsel),
                pl.BlockSpec((1, 1, D, D_EXPERT), wsel),
                pl.BlockSpec((1, 1, D_EXPERT, D), wsel),
            ],
            out_specs=pl.BlockSpec((TILE_M, D), lambda i, te, nu: (i, 0)),
            scratch_shapes=[
                pltpu.VMEM((D, D_EXPERT), BF16),
                pltpu.VMEM((D, D_EXPERT), BF16),
                pltpu.VMEM((D_EXPERT, D), BF16),
            ],
        ),
        out_shape=jax.ShapeDtypeStruct((P_ROWS, D), F32),
        compiler_params=_cparams(),
        name="expert_ffn",
    )(tile_e, n_used, xs, w1, w3, w2)


def _gather_issue(pos_ref, y_hbm, ybuf, sem, tile, slot):
    base = tile * TILE_C

    def issue(c, carry):
        for j in range(8):
            t = c * 8 + j
            for k in range(2):
                pltpu.make_async_copy(y_hbm.at[pl.ds(pos_ref[k * T + base + t], 1), :],
                                      ybuf.at[slot, k, c, pl.ds(j, 1), :], sem.at[slot]).start(priority=k)
        return carry

    lax.fori_loop(0, TILE_C // 8, issue, 0)


def _gather_wait(y_hbm, ybuf, sem, slot):
    src = y_hbm.at[pl.ds(0, TILE_C), :]
    for k in range(2):
        pltpu.make_async_copy(src, src, sem.at[slot]).wait()


def _gather_issue_inline(pos_ref, y_hbm, ybuf, sem, tile, slot):
    base = tile * TILE_C
    for t in range(TILE_C):
        for k in range(2):
            pltpu.make_async_copy(y_hbm.at[pl.ds(pos_ref[k * T + base + t], 1), :],
                                  ybuf.at[slot, k, t // 8, pl.ds(t % 8, 1), :],
                                  sem.at[slot]).start(priority=k)


def _combined(pos_ref, y_hbm, x_ref, meta_ref, ybuf, sem):
    i = pl.program_id(0)
    n = pl.num_programs(0)
    slot = i % 2

    @pl.when(i == 0)
    def _():
        _gather_issue(pos_ref, y_hbm, ybuf, sem, 0, 0)

    _gather_wait(y_hbm, ybuf, sem, slot)
    _gather_issue_inline(pos_ref, y_hbm, ybuf, sem, jnp.minimum(i + 1, n - 1), 1 - slot)
    mt = jnp.concatenate([meta_ref[...], jnp.zeros((120, TILE_C), F32)], axis=0).T
    y0 = ybuf[slot, 0].reshape(TILE_C, D)
    y1 = ybuf[slot, 1].reshape(TILE_C, D)
    return x_ref[...] + mt[:, 2:3] * y0 + mt[:, 3:4] * y1


def _combined_drain(y_hbm, ybuf, sem):
    i = pl.program_id(0)

    @pl.when(i == pl.num_programs(0) - 1)
    def _():
        _gather_wait(y_hbm, ybuf, sem, 1 - i % 2)


def _combine_kvq_body(pos_ref, y_hbm, x_ref, meta_ref, gkv_ref, wkv_ref, bkv_ref, gq_ref, wq_ref, bq_ref,
                      x2_ref, kv_ref, q_ref, ybuf, sem):
    x2 = _combined(pos_ref, y_hbm, x_ref, meta_ref, ybuf, sem)
    x2_ref[...] = x2
    kv_ref[...] = _dot(_rms(x2, gkv_ref[...]).astype(BF16), wkv_ref[...]) + bkv_ref[...]
    q_ref[...] = _dot(_rms(x2, gq_ref[...]).astype(BF16), wq_ref[...]) + bq_ref[...]
    _combined_drain(y_hbm, ybuf, sem)


def _combine_kvq(pos, ys, x1, meta, gkv, wkv, bkv, gq, wq, bq):
    row = lambda i, p: (i, 0)
    consts = [gkv, wkv, bkv, gq, wq, bq]
    return pl.pallas_call(
        _combine_kvq_body,
        grid_spec=pltpu.PrefetchScalarGridSpec(
            num_scalar_prefetch=1,
            grid=(T // TILE_C,),
            in_specs=[pl.BlockSpec(memory_space=pl.ANY),
                      pl.BlockSpec((TILE_C, D), row),
                      pl.BlockSpec((8, TILE_C), lambda i, p: (0, i))]
                     + [pl.BlockSpec(c.shape, lambda i, p, nd=c.ndim: (0,) * nd) for c in consts],
            out_specs=[pl.BlockSpec((TILE_C, D), row),
                       pl.BlockSpec((TILE_C, 2 * KVW), row),
                       pl.BlockSpec((TILE_C, D), row)],
            scratch_shapes=[pltpu.VMEM((2, 2, TILE_C // 8, 8, D), F32), pltpu.SemaphoreType.DMA((2,))],
        ),
        out_shape=[jax.ShapeDtypeStruct((T, D), F32),
                   jax.ShapeDtypeStruct((T, 2 * KVW), F32),
                   jax.ShapeDtypeStruct((T, D), F32)],
        compiler_params=_cparams(),
        name="combine_kvq",
    )(pos, ys, x1, meta, *consts)


def _combine_final_body(pos_ref, y_hbm, x_ref, meta_ref, gf_ref, yp_ref, ys_ref, ybuf, sem):
    i = pl.program_id(0)
    out = _rms(_combined(pos_ref, y_hbm, x_ref, meta_ref, ybuf, sem), gf_ref[...])

    @pl.when(i < TP // TILE_C)
    def _():
        yp_ref[...] = out

    @pl.when(i >= TP // TILE_C)
    def _():
        ys_ref[...] = out

    _combined_drain(y_hbm, ybuf, sem)


def _combine_final(pos, ys, x3, meta, gf):
    row = lambda i, p: (i, 0)
    npt = TP // TILE_C
    return pl.pallas_call(
        _combine_final_body,
        grid_spec=pltpu.PrefetchScalarGridSpec(
            num_scalar_prefetch=1,
            grid=(T // TILE_C,),
            in_specs=[pl.BlockSpec(memory_space=pl.ANY),
                      pl.BlockSpec((TILE_C, D), row),
                      pl.BlockSpec((8, TILE_C), lambda i, p: (0, i)),
                      pl.BlockSpec((1, D), lambda i, p: (0, 0))],
            out_specs=[pl.BlockSpec((TILE_C, D), lambda i, p: (jnp.minimum(i, npt - 1), 0)),
                       pl.BlockSpec((TILE_C, D), lambda i, p: (jnp.maximum(i - npt, 0), 0))],
            scratch_shapes=[pltpu.VMEM((2, 2, TILE_C // 8, 8, D), F32), pltpu.SemaphoreType.DMA((2,))],
        ),
        out_shape=[jax.ShapeDtypeStruct((TP, D), F32), jax.ShapeDtypeStruct((TS, D), F32)],
        compiler_params=_cparams(),
        name="combine_final",
    )(pos, ys, x3, meta, gf)


ATT_SUB = 4
ATT_GROUP = 8
TILE_A = ATT_SUB * WINDOW
NB_ATT = SEQ // TILE_A
NP_ATT = TP // TILE_A


def _attn_mixer_body(q_ref, kvc_ref, kvp_ref, x_ref, os_ref, bm_ref, band_ref, sink_ref,
                     wo_ref, bo_ref, gffn_ref, a_ref, rbias_ref, u_ref,
                     x3_ref, h_ref, meta_ref, cnt_out_ref, obuf, cnt_scr):
    i = pl.program_id(0)
    is_prompt = i < NP_ATT
    j = i % NB_ATT

    @pl.when(i == 0)
    def _():
        cnt_scr[...] = jnp.zeros_like(cnt_scr)

    @pl.when(is_prompt)
    def _():
        kall = jnp.concatenate([kvp_ref[:, :KVW], kvc_ref[:, :KVW]], axis=0).astype(BF16)
        vall = jnp.concatenate([kvp_ref[:, KVW:], kvc_ref[:, KVW:]], axis=0).astype(BF16)
        bm = bm_ref[...]
        bm16 = bm.astype(BF16)
        kbs = [kall[sb * WINDOW:(sb + 2) * WINDOW] for sb in range(ATT_SUB)]
        vbs = [vall[sb * WINDOW:(sb + 2) * WINDOW] for sb in range(ATT_SUB)]
        qbs = [q_ref[sb * WINDOW:(sb + 1) * WINDOW, :].astype(BF16) for sb in range(ATT_SUB)]
        valids = [band_ref[jnp.minimum(j, 1) if sb == 0 else 1] > 0.0 for sb in range(ATT_SUB)]
        all_units = [(sb, r) for sb in range(ATT_SUB) for r in range(GQA)]
        for u0 in range(0, len(all_units), ATT_GROUP):
            units = all_units[u0:u0 + ATT_GROUP]
            scores = []
            for sb, r in units:
                qm = jnp.concatenate([qbs[sb][:, r * KVW:(r + 1) * KVW]] * N_KV, axis=0) * bm16
                scores.append(_dot_nt(qm, kbs[sb]))
            probs, scales = [], []
            for (sb, r), s in zip(units, scores):
                s = jnp.where(valids[sb], s, MASK_VALUE)
                sink = sink_ref[r * N_KV * WINDOW:(r + 1) * N_KV * WINDOW]
                m = jnp.maximum(jnp.max(s, axis=-1, keepdims=True), sink)
                p = jnp.exp(s - m)
                den = jnp.sum(p, axis=-1, keepdims=True) + jnp.exp(sink - m)
                probs.append(p.astype(BF16))
                scales.append(1.0 / den)
            outs = [_dot(p, vbs[sb]) for (sb, r), p in zip(units, probs)]
            for (sb, r), o, sc in zip(units, outs, scales):
                o = o * (bm * sc)
                acc = o[0:WINDOW]
                for g in range(1, N_KV):
                    acc = acc + o[g * WINDOW:(g + 1) * WINDOW]
                obuf[sb * WINDOW:(sb + 1) * WINDOW, r * KVW:(r + 1) * KVW] = acc

    @pl.when(jnp.logical_not(is_prompt))
    def _():
        obuf[...] = os_ref[...]

    x3 = x_ref[...] + _dot(obuf[...].astype(BF16), wo_ref[...]) + bo_ref[...]
    x3_ref[...] = x3
    h, meta = _moe_prologue(x3, gffn_ref[...], a_ref, rbias_ref, u_ref, cnt_scr)
    h_ref[...] = h
    meta_ref[...] = meta
    cnt_out_ref[...] = cnt_scr[...]


def _attn_mixer(q, kv, x2, o_s, bm, band, sink_col, wo, bo, gffn, a, rbias, u):
    consts = [bm, band, sink_col, wo, bo, gffn, a, rbias, u]
    prow = lambda i: (jnp.minimum(i, NP_ATT - 1), 0)

    def prev(i):
        ic = jnp.minimum(i, NP_ATT - 1)
        return (ATT_SUB * ic - jnp.where(ic % NB_ATT == 0, 0, 1), 0)

    row = lambda i: (i, 0)
    return pl.pallas_call(
        _attn_mixer_body,
        grid=(T // TILE_A,),
        in_specs=[pl.BlockSpec((TILE_A, D), prow),
                  pl.BlockSpec((TILE_A, 2 * KVW), prow),
                  pl.BlockSpec((WINDOW, 2 * KVW), prev),
                  pl.BlockSpec((TILE_A, D), row),
                  pl.BlockSpec((TILE_A, D), lambda i: (jnp.maximum(i - NP_ATT, 0), 0))]
                 + [_const_spec(c.shape) for c in consts],
        out_specs=[pl.BlockSpec((TILE_A, D), row),
                   pl.BlockSpec((TILE_A, D), row),
                   pl.BlockSpec((8, TILE_A), lambda i: (0, i)),
                   _const_spec((N_EXPERTS, 128))],
        out_shape=[jax.ShapeDtypeStruct((T, D), F32),
                   jax.ShapeDtypeStruct((T, D), F32),
                   jax.ShapeDtypeStruct((8, T), F32),
                   jax.ShapeDtypeStruct((N_EXPERTS, 128), F32)],
        scratch_shapes=[pltpu.VMEM((TILE_A, D), F32), pltpu.VMEM((N_EXPERTS, 128), F32)],
        compiler_params=_cparams(),
        name="attn_mixer",
    )(q, kv, kv, x2, o_s, *consts)


def _attn_sample_body(q_ref, kvn_ref, ck_ref, cv_ref, bm_ref, vc_ref, vn_ref, sink_ref,
                      o_ref, nk_ref, nv_ref):
    bm = bm_ref[...]
    valid_c = vc_ref[...] > 0.0
    valid_n = vn_ref[...] > 0.0
    sink = sink_ref[...]
    hist = WINDOW - DEC_SEQ

    def one(b, carry):
        q8 = q_ref[b]
        kvn = kvn_ref[b]
        qm = jnp.concatenate(
            [q8[:, r * KVW:(r + 1) * KVW] for r in range(GQA) for _ in range(N_KV)], axis=0)
        qm = (qm * bm).astype(BF16)
        s_c = jnp.where(valid_c, _dot_nt(qm, ck_ref[b].astype(BF16)), MASK_VALUE)
        s_n = jnp.where(valid_n, _dot_nt(qm, kvn[:, :KVW].astype(BF16)), MASK_VALUE)
        m = jnp.maximum(jnp.maximum(jnp.max(s_c, axis=-1, keepdims=True),
                                    jnp.max(s_n, axis=-1, keepdims=True)), sink)
        p_c = jnp.exp(s_c - m)
        p_n = jnp.exp(s_n - m)
        den = (jnp.sum(p_c, axis=-1, keepdims=True) + jnp.sum(p_n, axis=-1, keepdims=True)
               + jnp.exp(sink - m))
        o = (_dot(p_c.astype(BF16), cv_ref[b].astype(BF16))
             + _dot(p_n.astype(BF16), kvn[:, KVW:].astype(BF16))) * (bm * (1.0 / den))
        outs = []
        for r in range(GQA):
            acc = o[r * 32:r * 32 + 8]
            for g in range(1, N_KV):
                acc = acc + o[r * 32 + g * 8:r * 32 + g * 8 + 8]
            outs.append(acc)
        o_ref[b] = jnp.concatenate(outs, axis=1)
        nk_ref[b, 0:hist, :] = ck_ref[b, DEC_SEQ:WINDOW, :]
        nk_ref[b, hist:WINDOW, :] = kvn_ref[b, 0:DEC_SEQ, 0:KVW]
        nv_ref[b, 0:hist, :] = cv_ref[b, DEC_SEQ:WINDOW, :]
        nv_ref[b, hist:WINDOW, :] = kvn_ref[b, 0:DEC_SEQ, KVW:2 * KVW]
        return carry

    lax.fori_loop(0, SAMPLE_BB, one, 0, unroll=2)


def _attn_sample(q8, kvn16, ck, cv, bm, vc, vn, sink_col8):
    bb = SAMPLE_BB
    blk = lambda i: (i, 0, 0)
    consts = [bm, vc, vn, sink_col8]
    return pl.pallas_call(
        _attn_sample_body,
        grid=(DEC_BATCH // bb,),
        in_specs=[pl.BlockSpec((bb, 8, D), blk),
                  pl.BlockSpec((bb, NKN, 2 * KVW), blk),
                  pl.BlockSpec((bb, WINDOW, KVW), blk),
                  pl.BlockSpec((bb, WINDOW, KVW), blk)] + [_const_spec(c.shape) for c in consts],
        out_specs=[pl.BlockSpec((bb, 8, D), blk),
                   pl.BlockSpec((bb, WINDOW, KVW), blk),
                   pl.BlockSpec((bb, WINDOW, KVW), blk)],
        out_shape=[jax.ShapeDtypeStruct((DEC_BATCH, 8, D), F32),
                   jax.ShapeDtypeStruct((DEC_BATCH, WINDOW, KVW), F32),
                   jax.ShapeDtypeStruct((DEC_BATCH, WINDOW, KVW), F32)],
        compiler_params=_cparams(),
        name="attn_sample",
    )(q8, kvn16, ck, cv, *consts)


def _router_weights(w_group, b_group, w_router, b_router):
    wt = jnp.zeros((N_LOGIT_ROWS, D), F32)
    wt = wt.at[0:N_GROUPS].set(w_group.T)
    wt = wt.at[8:8 + N_EXPERTS].set(jnp.transpose(w_router, (0, 2, 1)).reshape(N_EXPERTS, D))
    hi = wt.astype(BF16)
    lo = (wt - hi.astype(F32)).astype(BF16)
    a = jnp.concatenate([hi, lo], axis=0)
    bias = jnp.zeros((N_LOGIT_ROWS,), F32)
    bias = bias.at[0:N_GROUPS].set(b_group)
    bias = bias.at[N_GROUPS:8].set(MASK_VALUE)
    bias = bias.at[8:8 + N_EXPERTS].set(b_router.reshape(N_EXPERTS))
    return a, jnp.broadcast_to(bias[:, None], (N_LOGIT_ROWS, 128))


def _routing_tables(cnt):
    counts = cnt[:, 0].astype(jnp.int32)
    padded = ((counts + TILE_M - 1) // TILE_M) * TILE_M
    ends = jnp.cumsum(padded)
    offs = ends - padded
    tile_start = jnp.arange(N_TILES, dtype=jnp.int32) * TILE_M
    n_used = (ends[-1] // TILE_M).astype(jnp.int32)
    tile_e = jnp.sum(tile_start[:, None] >= ends[None, :], axis=1).astype(jnp.int32)
    last_e = tile_e[jnp.maximum(n_used - 1, 0)]
    tile_e = jnp.where(tile_start < ends[-1], tile_e, last_e)
    zstart = jnp.concatenate([(offs + counts) // 8 * 8, ends[-1:]])
    zend = jnp.concatenate([ends, jnp.full((1,), P_ROWS, jnp.int32)])
    pad = jnp.concatenate([zstart, (zend - zstart) // 8]).astype(jnp.int32)
    return offs.astype(jnp.int32), pad, tile_e, n_used.reshape(1)


def _moe_layer(layer, h, meta, cnt, w1, w3, w2):
    offs, pad, tile_e, n_used = _routing_tables(cnt)
    pos = _positions(offs, meta).reshape(N_ASSIGN)
    xs = _scatter_rows(pos, pad, h)
    ys = _expert_ffn(layer, tile_e, n_used, xs, w1, w3, w2)
    return pos, ys


def _heads_r_major(w):
    return jnp.transpose(w.reshape(N_KV, GQA, HEAD_DIM, -1), (1, 0, 2, 3)).reshape(D, -1)


def _block_mask(rows_per_block, n_rep):
    n = n_rep * N_KV * rows_per_block
    r = (jnp.arange(n)[:, None] // rows_per_block) % N_KV
    c = jnp.arange(KVW)[None, :] // HEAD_DIM
    return (r == c).astype(F32)


def kernel(x_prompt, x_sample, state_conv, cache_k, cache_v, norm_mix, norm_ffn, conv_w_in, conv_b_in, conv_w_dw, conv_b_dw, conv_ln_g, conv_ln_b, conv_w_out, conv_b_out, norm_kv, w_kv, b_kv, attn_w_q, attn_b_q, attn_sinks, attn_w_o, attn_b_o, moe_w_group, moe_b_group, moe_w_router, moe_b_router, moe_w1, moe_w3, moe_w2, norm_final):
    r2 = lambda v: v.reshape(1, -1)
    n_tri = max(TILE_SEQ, TILE_A)
    u_tri = jnp.triu(jnp.ones((n_tri, n_tri), BF16), 1)
    ar0, rb0 = _router_weights(moe_w_group[0], moe_b_group[0], moe_w_router[0], moe_b_router[0])
    ar1, rb1 = _router_weights(moe_w_group[1], moe_b_group[1], moe_w_router[1], moe_b_router[1])

    win = conv_w_in[0].astype(BF16)
    wout = conv_w_out[0].astype(BF16)
    wdw = conv_w_dw[0]
    xs2 = x_sample.reshape(TS, D)
    u_s = _glu_sample(xs2, r2(norm_mix[0]), win, r2(conv_b_in[0]))
    tt = jnp.arange(DEC_SEQ)[:, None]
    jh = jnp.arange(CONV_HIST)[None, :]
    wh = jnp.where((jh >= tt)[..., None], wdw[jnp.clip(jh - tt, 0, CONV_W - 1)], 0.0)
    ju = jnp.arange(DEC_SEQ)[None, :]
    wu = jnp.where((ju <= tt)[..., None], wdw[jnp.clip(CONV_HIST - tt + ju, 0, CONV_W - 1)], 0.0)
    y_s, state_s = _dwconv_sample(u_s.reshape(DEC_BATCH, DEC_SEQ, D), state_conv, wh, wu)
    wdw8 = jnp.broadcast_to(wdw[:, None, :], (CONV_W, 8, D))
    x1, h, meta, cnt, state_p = _conv_mixer(
        x_prompt.reshape(TP, D), xs2, y_s.reshape(TS, D), r2(norm_mix[0]), win, r2(conv_b_in[0]), wdw8,
        r2(conv_b_dw[0]), r2(conv_ln_g[0]), r2(conv_ln_b[0]), wout, r2(conv_b_out[0]), r2(norm_ffn[0]),
        ar0, rb0, u_tri[:TILE_SEQ, :TILE_SEQ])

    pos, ys = _moe_layer(0, h, meta, cnt, moe_w1, moe_w3, moe_w2)
    scale = HEAD_DIM ** -0.5
    wq4 = attn_w_q[0].reshape(D, N_KV, GQA, HEAD_DIM)
    wq = (jnp.transpose(wq4, (0, 2, 1, 3)).reshape(D, D) * scale).astype(BF16)
    bq = r2(_heads_r_major(attn_b_q[0][:, None])[:, 0] * scale)
    wo = _heads_r_major(attn_w_o[0]).astype(BF16)
    x2, kv, q = _combine_kvq(pos, ys, x1, meta, r2(norm_kv), w_kv.astype(BF16), r2(b_kv),
                             r2(norm_mix[1]), wq, bq)

    sinks = attn_sinks[0].astype(F32).reshape(N_KV, GQA).T
    sink_col = jnp.repeat(sinks.reshape(-1), WINDOW)[:, None]
    sink_col8 = jnp.repeat(sinks.reshape(-1), 8)[:, None]
    qi = (jnp.arange(GQA * N_KV * 8) % 8)[:, None]
    vc = ((qi < DEC_SEQ) & (jnp.arange(WINDOW)[None, :] > qi)).astype(F32)
    vn = ((qi < DEC_SEQ) & (jnp.arange(NKN)[None, :] <= qi)).astype(F32)
    q8 = jnp.pad(q[TP:].reshape(DEC_BATCH, DEC_SEQ, D), ((0, 0), (0, 8 - DEC_SEQ), (0, 0)))
    kvn = jnp.pad(kv[TP:].reshape(DEC_BATCH, DEC_SEQ, 2 * KVW), ((0, 0), (0, NKN - DEC_SEQ), (0, 0)))
    o_s8, nk_s, nv_s = _attn_sample(q8, kvn, cache_k.reshape(DEC_BATCH, WINDOW, KVW),
                                    cache_v.reshape(DEC_BATCH, WINDOW, KVW),
                                    _block_mask(8, GQA), vc, vn, sink_col8)
    o_s = o_s8[:, :DEC_SEQ].reshape(TS, D)
    pq = (jnp.arange(N_KV * WINDOW) % WINDOW)[:, None]
    pk = jnp.arange(2 * WINDOW)[None, :]
    dist = pq + WINDOW - pk
    band_rest = (dist >= 0) & (dist < WINDOW)
    band = jnp.stack([band_rest & (pk >= WINDOW), band_rest]).astype(F32)
    x3, h, meta, cnt = _attn_mixer(q, kv, x2, o_s, _block_mask(WINDOW, 1), band, sink_col, wo,
                                   r2(attn_b_o[0]), r2(norm_ffn[1]), ar1, rb1, u_tri[:TILE_A, :TILE_A])

    pos, ys = _moe_layer(1, h, meta, cnt, moe_w1, moe_w3, moe_w2)
    y_p, y_s2 = _combine_final(pos, ys, x3, meta, r2(norm_final))

    kvp = jnp.stack([kv[(b + 1) * SEQ - WINDOW:(b + 1) * SEQ] for b in range(BATCH)])
    new_k_p = kvp[..., :KVW].reshape(BATCH, WINDOW, N_KV, HEAD_DIM)
    new_v_p = kvp[..., KVW:].reshape(BATCH, WINDOW, N_KV, HEAD_DIM)
    return (y_p.reshape(BATCH, SEQ, D), y_s2.reshape(DEC_BATCH, DEC_SEQ, D),
            state_p[None], state_s, new_k_p, new_v_p,
            nk_s.reshape(DEC_BATCH, WINDOW, N_KV, HEAD_DIM), nv_s.reshape(DEC_BATCH, WINDOW, N_KV, HEAD_DIM))
```

```python
import jax
import jax.numpy as jnp
from jax import lax
from jax.experimental import pallas as pl
from jax.experimental.pallas import tpu as pltpu

D = 1024
BATCH = 8
SEQ = 2048
DEC_BATCH = 128
DEC_SEQ = 4
CONV_W = 31
CONV_HIST = CONV_W - 1
HEAD_DIM = 64
N_HEADS = 16
N_KV = 4
GQA = 4
KVW = N_KV * HEAD_DIM
WINDOW = 128
N_GROUPS = 4
EPG = 8
N_EXPERTS = 32
D_EXPERT = 512
RMS_EPS = 1e-5
LN_EPS = 1e-5
MASK_VALUE = -1e30

TP = BATCH * SEQ
TS = DEC_BATCH * DEC_SEQ
T = TP + TS
N_ASSIGN = 2 * T

TILE_SEQ = 512
TILE_M = 512
N_TILES = (N_ASSIGN + N_EXPERTS * (TILE_M - 1)) // TILE_M + 1
P_ROWS = N_TILES * TILE_M
TILE_C = 256
N_LOGIT_ROWS = 48
SAMPLE_BB = 16
NKN = 16
VMEM_LIMIT = 60 * 1024 * 1024

F32 = jnp.float32
BF16 = jnp.bfloat16


def _cparams(n_axes=1):
    return pltpu.CompilerParams(dimension_semantics=("arbitrary",) * n_axes,
                                vmem_limit_bytes=VMEM_LIMIT)


def _const_spec(shape):
    nd = len(shape)
    return pl.BlockSpec(shape, lambda *_: (0,) * nd)


def _rms(x, g):
    return x * lax.rsqrt(jnp.mean(x * x, axis=-1, keepdims=True) + RMS_EPS) * g


def _dot(a, b):
    return jnp.dot(a, b, preferred_element_type=F32)


def _dot_nt(a, b):
    return lax.dot_general(a, b, (((1,), (1,)), ((), ())), preferred_element_type=F32)


def _moe_prologue(x, g_ffn, a_ref, rbias_ref, u_ref, cnt_ref):
    n = x.shape[0]
    h = _rms(x, g_ffn)
    h_hi = h.astype(BF16)
    h_lo = (h - h_hi.astype(F32)).astype(BF16)
    a = a_ref[...]
    l1 = _dot_nt(a, h_hi)
    l2 = _dot_nt(a[:N_LOGIT_ROWS], h_lo)
    logits = l1[:N_LOGIT_ROWS] + l1[N_LOGIT_ROWS:] + l2 + rbias_ref[:, 0:1]
    iota8 = lax.broadcasted_iota(jnp.int32, (8, n), 0).astype(F32)
    gl = logits[0:8]
    gmax = jnp.max(gl, axis=0, keepdims=True)
    g_idx = jnp.min(jnp.where(gl == gmax, iota8, 8.0), axis=0, keepdims=True)
    g_w = 1.0 / jnp.sum(jnp.exp(gl - gmax), axis=0, keepdims=True)
    es = jnp.where(g_idx == 0.0, logits[8:16],
                   jnp.where(g_idx == 1.0, logits[16:24],
                             jnp.where(g_idx == 2.0, logits[24:32], logits[32:40])))
    v1 = jnp.max(es, axis=0, keepdims=True)
    i1 = jnp.min(jnp.where(es == v1, iota8, 8.0), axis=0, keepdims=True)
    es2 = jnp.where(iota8 == i1, -jnp.inf, es)
    v2 = jnp.max(es2, axis=0, keepdims=True)
    i2 = jnp.min(jnp.where(es2 == v2, iota8, 8.0), axis=0, keepdims=True)
    e2x = jnp.exp(v2 - v1)
    w1 = g_w / (1.0 + e2x)
    w2 = g_w * e2x / (1.0 + e2x)
    ex1 = g_idx * EPG + i1
    ex2 = g_idx * EPG + i2
    iota32 = lax.broadcasted_iota(jnp.int32, (N_EXPERTS, n), 0).astype(F32)
    oh1 = jnp.where(iota32 == ex1, 1.0, 0.0)
    oh2 = jnp.where(iota32 == ex2, 1.0, 0.0)
    onehot = oh1 + oh2
    before = _dot(onehot.astype(BF16), u_ref[...]) + cnt_ref[:, 0:1]
    rank1 = jnp.sum(oh1 * before, axis=0, keepdims=True)
    rank2 = jnp.sum(oh2 * before, axis=0, keepdims=True)
    cnt_ref[...] = cnt_ref[...] + jnp.sum(onehot, axis=1, keepdims=True)
    rows = (ex1, ex2, w1, w2, rank1, rank2)
    meta = jnp.zeros((8, n), F32)
    for r, val in enumerate(rows):
        meta = jnp.where(iota8 == r, val, meta)
    return h, meta


def _ln_silu(y, g, b):
    mu = jnp.mean(y, axis=-1, keepdims=True)
    yc = y - mu
    var = jnp.mean(yc * yc, axis=-1, keepdims=True)
    z = yc * lax.rsqrt(var + LN_EPS) * g + b
    return z * jax.nn.sigmoid(z)


CONV_CH = 32
NP_CONV = TP // TILE_SEQ
NSQ = SEQ // TILE_SEQ


def _conv_mixer_body(xp_ref, xs_ref, ys_ref, gmix_ref, win_ref, bin_ref, wdw_ref, bdw_ref,
                     lng_ref, lnb_ref, wout_ref, bout_ref, gffn_ref, a_ref, rbias_ref, u_ref,
                     x1_ref, h_ref, meta_ref, cnt_out_ref, state_ref,
                     ubuf, ush, ybuf, cnt_scr):
    i = pl.program_id(0)
    is_prompt = i < NP_CONV
    s = i % NSQ

    @pl.when(i == 0)
    def _():
        cnt_scr[...] = jnp.zeros_like(cnt_scr)

    @pl.when(is_prompt)
    def _():
        @pl.when(s == 0)
        def _():
            ubuf[0:32, :] = jnp.zeros((32, D), F32)

        hn = _rms(xp_ref[...], gmix_ref[...])
        u2 = _dot(hn.astype(BF16), win_ref[...]) + bin_ref[...]
        ubuf[32:32 + TILE_SEQ, :] = u2[:, :D] * jax.nn.sigmoid(u2[:, D:])

        for sh in range(1, 8):
            ush[sh - 1] = ubuf[sh:sh + TILE_SEQ + 24, :]

        ngrp = CONV_CH // 8

        def chunk(c, carry):
            r0 = pl.multiple_of(c * CONV_CH, CONV_CH)
            for lt in range(D // 128):
                lanes = slice(lt * 128, (lt + 1) * 128)
                acc = [None] * ngrp
                for sh in range(8):
                    a8s = [a8 for a8 in range(5) if 0 <= 8 * a8 + sh - 2 < CONV_W]
                    win = {}
                    for gi in range(a8s[0], a8s[-1] + ngrp):
                        start = pl.multiple_of(r0 + 8 * gi, 8)
                        if sh == 0:
                            win[gi] = ubuf[pl.ds(start, 8), lanes]
                        else:
                            win[gi] = ush[sh - 1, pl.ds(start, 8), lanes]
                    for a8 in a8s:
                        w = wdw_ref[8 * a8 + sh - 2, :, lanes]
                        for gq in range(ngrp):
                            term = w * win[a8 + gq]
                            acc[gq] = term if acc[gq] is None else acc[gq] + term
                for gq in range(ngrp):
                    ybuf[pl.ds(pl.multiple_of(r0 + 8 * gq, 8), 8), lanes] = acc[gq]
            return carry

        lax.fori_loop(0, TILE_SEQ // CONV_CH, chunk, 0)

        @pl.when(s == NSQ - 1)
        def _():
            state_ref[0] = ubuf[TILE_SEQ + 2:TILE_SEQ + 32, :]

        ubuf[0:32, :] = ubuf[TILE_SEQ:TILE_SEQ + 32, :]

    @pl.when(jnp.logical_not(is_prompt))
    def _():
        ybuf[...] = ys_ref[...]

    x = jnp.where(is_prompt, xp_ref[...], xs_ref[...])
    act = _ln_silu(ybuf[...] + bdw_ref[...], lng_ref[...], lnb_ref[...])
    x1 = x + _dot(act.astype(BF16), wout_ref[...]) + bout_ref[...]
    x1_ref[...] = x1
    h, meta = _moe_prologue(x1, gffn_ref[...], a_ref, rbias_ref, u_ref, cnt_scr)
    h_ref[...] = h
    meta_ref[...] = meta
    cnt_out_ref[...] = cnt_scr[...]


def _conv_mixer(xp, xs, ys, gmix, win, b_in, wdw, bdw, lng, lnb, wout, bout, gffn, a, rbias, u):
    consts = [gmix, win, b_in, wdw, bdw, lng, lnb, wout, bout, gffn, a, rbias, u]
    prow = lambda i: (jnp.minimum(i, NP_CONV - 1), 0)
    srow = lambda i: (jnp.maximum(i - NP_CONV, 0), 0)
    row = lambda i: (i, 0)
    return pl.pallas_call(
        _conv_mixer_body,
        grid=(T // TILE_SEQ,),
        in_specs=[pl.BlockSpec((TILE_SEQ, D), prow),
                  pl.BlockSpec((TILE_SEQ, D), srow),
                  pl.BlockSpec((TILE_SEQ, D), srow)] + [_const_spec(c.shape) for c in consts],
        out_specs=[
            pl.BlockSpec((TILE_SEQ, D), row),
            pl.BlockSpec((TILE_SEQ, D), row),
            pl.BlockSpec((8, TILE_SEQ), lambda i: (0, i)),
            _const_spec((N_EXPERTS, 128)),
            pl.BlockSpec((1, CONV_HIST, D), lambda i: (jnp.minimum(i // NSQ, BATCH - 1), 0, 0)),
        ],
        out_shape=[
            jax.ShapeDtypeStruct((T, D), F32),
            jax.ShapeDtypeStruct((T, D), F32),
            jax.ShapeDtypeStruct((8, T), F32),
            jax.ShapeDtypeStruct((N_EXPERTS, 128), F32),
            jax.ShapeDtypeStruct((BATCH, CONV_HIST, D), F32),
        ],
        scratch_shapes=[
            pltpu.VMEM((TILE_SEQ + 32, D), F32),
            pltpu.VMEM((7, TILE_SEQ + 24, D), F32),
            pltpu.VMEM((TILE_SEQ, D), F32),
            pltpu.VMEM((N_EXPERTS, 128), F32),
        ],
        compiler_params=_cparams(),
        name="conv_mixer",
    )(xp, xs, ys, *consts)


def _glu_sample_body(x_ref, gmix_ref, win_ref, bin_ref, u_ref):
    hn = _rms(x_ref[...], gmix_ref[...])
    u2 = _dot(hn.astype(BF16), win_ref[...]) + bin_ref[...]
    u_ref[...] = u2[:, :D] * jax.nn.sigmoid(u2[:, D:])


def _glu_sample(x, gmix, win, b_in):
    args = [x, gmix, win, b_in]
    return pl.pallas_call(
        _glu_sample_body,
        grid=(1,),
        in_specs=[_const_spec(a.shape) for a in args],
        out_specs=_const_spec((TS, D)),
        out_shape=jax.ShapeDtypeStruct((TS, D), F32),
        compiler_params=_cparams(),
        name="glu_sample",
    )(*args)


def _dwconv_sample_body(u_ref, st_ref, wh_ref, wu_ref, y_ref, nst_ref):
    st = st_ref[0]
    u = u_ref[...]
    for t in range(DEC_SEQ):
        y_ref[:, t:t + 1, :] = (jnp.sum(st * wh_ref[t][None], axis=1, keepdims=True)
                                + jnp.sum(u * wu_ref[t][None], axis=1, keepdims=True))
    nst_ref[0, :, 0:CONV_HIST - DEC_SEQ, :] = st_ref[0, :, DEC_SEQ:CONV_HIST, :]
    nst_ref[0, :, CONV_HIST - DEC_SEQ:CONV_HIST, :] = u


def _dwconv_sample(u3, state, wh, wu):
    bb = SAMPLE_BB
    blk = lambda i: (i, 0, 0)
    sblk = lambda i: (0, i, 0, 0)
    return pl.pallas_call(
        _dwconv_sample_body,
        grid=(DEC_BATCH // bb,),
        in_specs=[
            pl.BlockSpec((bb, DEC_SEQ, D), blk),
            pl.BlockSpec((1, bb, CONV_HIST, D), sblk),
            _const_spec(wh.shape), _const_spec(wu.shape),
        ],
        out_specs=[pl.BlockSpec((bb, DEC_SEQ, D), blk), pl.BlockSpec((1, bb, CONV_HIST, D), sblk)],
        out_shape=[
            jax.ShapeDtypeStruct((DEC_BATCH, DEC_SEQ, D), F32),
            jax.ShapeDtypeStruct((1, DEC_BATCH, CONV_HIST, D), F32),
        ],
        compiler_params=_cparams(),
        name="dwconv_sample",
    )(u3, state, wh, wu)


def _positions_body(offs_ref, meta_ref, pos_ref):
    ex = meta_ref[0:2, :]
    acc = meta_ref[4:6, :]
    for e in range(N_EXPERTS):
        acc = acc + jnp.where(ex == float(e), offs_ref[e].astype(F32), 0.0)
    pos_ref[...] = acc.astype(jnp.int32)


def _positions(offs, meta):
    return pl.pallas_call(
        _positions_body,
        grid_spec=pltpu.PrefetchScalarGridSpec(
            num_scalar_prefetch=1,
            grid=(1,),
            in_specs=[pl.BlockSpec((8, T), lambda i, o: (0, 0))],
            out_specs=pl.BlockSpec((2, T), lambda i, o: (0, 0)),
        ),
        out_shape=jax.ShapeDtypeStruct((2, T), jnp.int32),
        compiler_params=_cparams(),
        name="positions",
    )(offs, meta)


def _inverse_body(pos_ref, pad_ref, src_ref):
    def clear(e, carry):
        start = pad_ref[e]

        def chunk(c, cc):
            for j in range(8):
                src_ref[start + c * 8 + j] = 0
            return cc

        return lax.fori_loop(0, pad_ref[N_EXPERTS + 1 + e], chunk, carry)

    lax.fori_loop(0, N_EXPERTS + 1, clear, 0)

    def fill(c, carry):
        for j in range(8):
            t = c * 8 + j
            src_ref[pos_ref[t]] = t
            src_ref[pos_ref[T + t]] = t
        return carry

    lax.fori_loop(0, T // 8, fill, 0)


def _inverse_positions(pos, pad):
    return pl.pallas_call(
        _inverse_body,
        grid_spec=pltpu.PrefetchScalarGridSpec(
            num_scalar_prefetch=2,
            grid=(1,),
            in_specs=[],
            out_specs=pl.BlockSpec(memory_space=pltpu.SMEM),
        ),
        out_shape=jax.ShapeDtypeStruct((P_ROWS,), jnp.int32),
        compiler_params=_cparams(),
        name="inverse_positions",
    )(pos, pad)


def _expert_gather_wait(h_hbm, sem, slot):
    src = h_hbm.at[pl.ds(0, TILE_M), :]
    pltpu.make_async_copy(src, src, sem.at[slot]).wait()


def _expert_body(te_ref, nu_ref, src_ref, h_hbm, w1_ref, w3_ref, w2_ref, y_ref,
                 xbuf_a, xbuf_b, w1b, w3b, w2b, sem):
    i = pl.program_id(0)
    nu = nu_ref[0]
    live = i < nu
    prev = te_ref[jnp.maximum(i - 1, 0)]
    fresh = (i == 0) | (te_ref[i] != prev)
    xbufs = (xbuf_a, xbuf_b)

    @pl.when(i == 0)
    def _():
        def issue(c, carry):
            for j in range(8):
                t = c * 8 + j
                pltpu.make_async_copy(h_hbm.at[pl.ds(src_ref[t], 1), :],
                                      xbuf_a.at[c, pl.ds(j, 1), :], sem.at[0]).start(priority=j % 2)
            return carry

        lax.fori_loop(0, TILE_M // 8, issue, 0)

    @pl.when(live & fresh)
    def _():
        w1b[...] = w1_ref[0, 0].astype(BF16)
        w3b[...] = w3_ref[0, 0].astype(BF16)
        w2b[...] = w2_ref[0, 0].astype(BF16)

    for slot in range(2):
        @pl.when(live & (i % 2 == slot))
        def _(slot=slot):
            _expert_gather_wait(h_hbm, sem, slot)
            base = jnp.minimum(i + 1, nu - 1) * TILE_M
            for t in range(TILE_M):
                pltpu.make_async_copy(h_hbm.at[pl.ds(src_ref[base + t], 1), :],
                                      xbufs[1 - slot].at[t // 8, pl.ds(t % 8, 1), :],
                                      sem.at[1 - slot]).start(priority=t % 2)
            xb = xbufs[slot][...].reshape(TILE_M, D).astype(BF16)
            a = _dot(xb, w1b[...])
            g = _dot(xb, w3b[...])
            act = a * jax.nn.sigmoid(a) * g
            y_ref[...] = _dot(act.astype(BF16), w2b[...])

            @pl.when(i == nu - 1)
            def _():
                _expert_gather_wait(h_hbm, sem, 1 - slot)

    @pl.when(jnp.logical_not(live))
    def _():
        y_ref[...] = jnp.zeros_like(y_ref)


def _expert_ffn(layer, tile_e, n_used, src, h, w1, w3, w2):
    wsel = lambda i, te, nu, sr: (layer, te[i], 0, 0)
    return pl.pallas_call(
        _expert_body,
        grid_spec=pltpu.PrefetchScalarGridSpec(
            num_scalar_prefetch=3,
            grid=(N_TILES,),
            in_specs=[
                pl.BlockSpec(memory_space=pl.ANY),
                pl.BlockSpec((1, 1, D, D_EXPERT), wsel),
                pl.BlockSpec((1, 1, D, D_EXPERT), wsel),
                pl.BlockSpec((1, 1, D_EXPERT, D), wsel),
            ],
            out_specs=pl.BlockSpec((TILE_M, D), lambda i, te, nu, sr: (i, 0)),
            scratch_shapes=[
                pltpu.VMEM((TILE_M // 8, 8, D), F32),
                pltpu.VMEM((TILE_M // 8, 8, D), F32),
                pltpu.VMEM((D, D_EXPERT), BF16),
                pltpu.VMEM((D, D_EXPERT), BF16),
                pltpu.VMEM((D_EXPERT, D), BF16),
                pltpu.SemaphoreType.DMA((2,)),
            ],
        ),
        out_shape=jax.ShapeDtypeStruct((P_ROWS, D), F32),
        compiler_params=_cparams(),
        name="expert_ffn",
    )(tile_e, n_used, src, h, w1, w3, w2)


def _gather_wait(y_hbm, sem, slot):
    src = y_hbm.at[pl.ds(0, TILE_C), :]
    for k in range(2):
        pltpu.make_async_copy(src, src, sem.at[slot]).wait()


def _combine_steps(pos_ref, y_hbm, x_ref, meta_ref, ybufs, sem, finish):
    i = pl.program_id(0)
    n = pl.num_programs(0)

    @pl.when(i == 0)
    def _():
        def issue(c, carry):
            for j in range(8):
                for k in range(2):
                    pltpu.make_async_copy(y_hbm.at[pl.ds(pos_ref[k * T + c * 8 + j], 1), :],
                                          ybufs[0].at[k, c, pl.ds(j, 1), :], sem.at[0]).start(priority=k)
            return carry

        lax.fori_loop(0, TILE_C // 8, issue, 0)

    for slot in range(2):
        @pl.when(i % 2 == slot)
        def _(slot=slot):
            _gather_wait(y_hbm, sem, slot)
            base = jnp.minimum(i + 1, n - 1) * TILE_C
            for t in range(TILE_C):
                for k in range(2):
                    pltpu.make_async_copy(y_hbm.at[pl.ds(pos_ref[k * T + base + t], 1), :],
                                          ybufs[1 - slot].at[k, t // 8, pl.ds(t % 8, 1), :],
                                          sem.at[1 - slot]).start(priority=k)
            mt = jnp.concatenate([meta_ref[...], jnp.zeros((120, TILE_C), F32)], axis=0).T
            y0 = ybufs[slot][0].reshape(TILE_C, D)
            y1 = ybufs[slot][1].reshape(TILE_C, D)
            finish(x_ref[...] + mt[:, 2:3] * y0 + mt[:, 3:4] * y1)

            @pl.when(i == n - 1)
            def _():
                _gather_wait(y_hbm, sem, 1 - slot)


def _combine_scratch():
    buf = pltpu.VMEM((2, TILE_C // 8, 8, D), F32)
    return [buf, buf, pltpu.SemaphoreType.DMA((2,))]


def _combine_kvq_body(pos_ref, y_hbm, x_ref, meta_ref, gkv_ref, wkv_ref, bkv_ref, gq_ref, wq_ref, bq_ref,
                      x2_ref, kv_ref, q_ref, ybuf_a, ybuf_b, sem):
    def finish(x2):
        x2_ref[...] = x2
        kv_ref[...] = _dot(_rms(x2, gkv_ref[...]).astype(BF16), wkv_ref[...]) + bkv_ref[...]
        q_ref[...] = _dot(_rms(x2, gq_ref[...]).astype(BF16), wq_ref[...]) + bq_ref[...]

    _combine_steps(pos_ref, y_hbm, x_ref, meta_ref, (ybuf_a, ybuf_b), sem, finish)


def _combine_kvq(pos, ys, x1, meta, gkv, wkv, bkv, gq, wq, bq):
    row = lambda i, p: (i, 0)
    consts = [gkv, wkv, bkv, gq, wq, bq]
    return pl.pallas_call(
        _combine_kvq_body,
        grid_spec=pltpu.PrefetchScalarGridSpec(
            num_scalar_prefetch=1,
            grid=(T // TILE_C,),
            in_specs=[pl.BlockSpec(memory_space=pl.ANY),
                      pl.BlockSpec((TILE_C, D), row),
                      pl.BlockSpec((8, TILE_C), lambda i, p: (0, i))]
                     + [pl.BlockSpec(c.shape, lambda i, p, nd=c.ndim: (0,) * nd) for c in consts],
            out_specs=[pl.BlockSpec((TILE_C, D), row),
                       pl.BlockSpec((TILE_C, 2 * KVW), row),
                       pl.BlockSpec((TILE_C, D), row)],
            scratch_shapes=_combine_scratch(),
        ),
        out_shape=[jax.ShapeDtypeStruct((T, D), F32),
                   jax.ShapeDtypeStruct((T, 2 * KVW), F32),
                   jax.ShapeDtypeStruct((T, D), F32)],
        compiler_params=_cparams(),
        name="combine_kvq",
    )(pos, ys, x1, meta, *consts)


def _combine_final_body(pos_ref, y_hbm, x_ref, meta_ref, gf_ref, yp_ref, ys_ref, ybuf_a, ybuf_b, sem):
    i = pl.program_id(0)

    def finish(x4):
        out = _rms(x4, gf_ref[...])

        @pl.when(i < TP // TILE_C)
        def _():
            yp_ref[...] = out

        @pl.when(i >= TP // TILE_C)
        def _():
            ys_ref[...] = out

    _combine_steps(pos_ref, y_hbm, x_ref, meta_ref, (ybuf_a, ybuf_b), sem, finish)


def _combine_final(pos, ys, x3, meta, gf):
    row = lambda i, p: (i, 0)
    npt = TP // TILE_C
    return pl.pallas_call(
        _combine_final_body,
        grid_spec=pltpu.PrefetchScalarGridSpec(
            num_scalar_prefetch=1,
            grid=(T // TILE_C,),
            in_specs=[pl.BlockSpec(memory_space=pl.ANY),
                      pl.BlockSpec((TILE_C, D), row),
                      pl.BlockSpec((8, TILE_C), lambda i, p: (0, i)),
                      pl.BlockSpec((1, D), lambda i, p: (0, 0))],
            out_specs=[pl.BlockSpec((TILE_C, D), lambda i, p: (jnp.minimum(i, npt - 1), 0)),
                       pl.BlockSpec((TILE_C, D), lambda i, p: (jnp.maximum(i - npt, 0), 0))],
            scratch_shapes=_combine_scratch(),
        ),
        out_shape=[jax.ShapeDtypeStruct((TP, D), F32), jax.ShapeDtypeStruct((TS, D), F32)],
        compiler_params=_cparams(),
        name="combine_final",
    )(pos, ys, x3, meta, gf)


ATT_SUB = 4
ATT_GROUP = 8
TILE_A = ATT_SUB * WINDOW
NB_ATT = SEQ // TILE_A
NP_ATT = TP // TILE_A


def _attn_mixer_body(q_ref, kvc_ref, kvp_ref, x_ref, os_ref, bm_ref, band_ref, sink_ref,
                     wo_ref, bo_ref, gffn_ref, a_ref, rbias_ref, u_ref,
                     x3_ref, h_ref, meta_ref, cnt_out_ref, obuf, cnt_scr):
    i = pl.program_id(0)
    is_prompt = i < NP_ATT
    j = i % NB_ATT

    @pl.when(i == 0)
    def _():
        cnt_scr[...] = jnp.zeros_like(cnt_scr)

    @pl.when(is_prompt)
    def _():
        kall = jnp.concatenate([kvp_ref[:, :KVW], kvc_ref[:, :KVW]], axis=0).astype(BF16)
        vall = jnp.concatenate([kvp_ref[:, KVW:], kvc_ref[:, KVW:]], axis=0).astype(BF16)
        bm = bm_ref[...]
        bm16 = bm.astype(BF16)
        kbs = [kall[sb * WINDOW:(sb + 2) * WINDOW] for sb in range(ATT_SUB)]
        vbs = [vall[sb * WINDOW:(sb + 2) * WINDOW] for sb in range(ATT_SUB)]
        qbs = [q_ref[sb * WINDOW:(sb + 1) * WINDOW, :].astype(BF16) for sb in range(ATT_SUB)]
        valids = [band_ref[jnp.minimum(j, 1) if sb == 0 else 1] > 0.0 for sb in range(ATT_SUB)]
        all_units = [(sb, r) for sb in range(ATT_SUB) for r in range(GQA)]
        for u0 in range(0, len(all_units), ATT_GROUP):
            units = all_units[u0:u0 + ATT_GROUP]
            scores = []
            for sb, r in units:
                qm = jnp.concatenate([qbs[sb][:, r * KVW:(r + 1) * KVW]] * N_KV, axis=0) * bm16
                scores.append(_dot_nt(qm, kbs[sb]))
            probs, scales = [], []
            for (sb, r), s in zip(units, scores):
                s = jnp.where(valids[sb], s, MASK_VALUE)
                sink = sink_ref[r * N_KV * WINDOW:(r + 1) * N_KV * WINDOW]
                m = jnp.maximum(jnp.max(s, axis=-1, keepdims=True), sink)
                p = jnp.exp(s - m)
                den = jnp.sum(p, axis=-1, keepdims=True) + jnp.exp(sink - m)
                probs.append(p.astype(BF16))
                scales.append(1.0 / den)
            outs = [_dot(p, vbs[sb]) for (sb, r), p in zip(units, probs)]
            for (sb, r), o, sc in zip(units, outs, scales):
                o = o * (bm * sc)
                acc = o[0:WINDOW]
                for g in range(1, N_KV):
                    acc = acc + o[g * WINDOW:(g + 1) * WINDOW]
                obuf[sb * WINDOW:(sb + 1) * WINDOW, r * KVW:(r + 1) * KVW] = acc

    @pl.when(jnp.logical_not(is_prompt))
    def _():
        obuf[...] = os_ref[...]

    x3 = x_ref[...] + _dot(obuf[...].astype(BF16), wo_ref[...]) + bo_ref[...]
    x3_ref[...] = x3
    h, meta = _moe_prologue(x3, gffn_ref[...], a_ref, rbias_ref, u_ref, cnt_scr)
    h_ref[...] = h
    meta_ref[...] = meta
    cnt_out_ref[...] = cnt_scr[...]


def _attn_mixer(q, kv, x2, o_s, bm, band, sink_col, wo, bo, gffn, a, rbias, u):
    consts = [bm, band, sink_col, wo, bo, gffn, a, rbias, u]
    prow = lambda i: (jnp.minimum(i, NP_ATT - 1), 0)

    def prev(i):
        ic = jnp.minimum(i, NP_ATT - 1)
        return (ATT_SUB * ic - jnp.where(ic % NB_ATT == 0, 0, 1), 0)

    row = lambda i: (i, 0)
    return pl.pallas_call(
        _attn_mixer_body,
        grid=(T // TILE_A,),
        in_specs=[pl.BlockSpec((TILE_A, D), prow),
                  pl.BlockSpec((TILE_A, 2 * KVW), prow),
                  pl.BlockSpec((WINDOW, 2 * KVW), prev),
                  pl.BlockSpec((TILE_A, D), row),
                  pl.BlockSpec((TILE_A, D), lambda i: (jnp.maximum(i - NP_ATT, 0), 0))]
                 + [_const_spec(c.shape) for c in consts],
        out_specs=[pl.BlockSpec((TILE_A, D), row),
                   pl.BlockSpec((TILE_A, D), row),
                   pl.BlockSpec((8, TILE_A), lambda i: (0, i)),
                   _const_spec((N_EXPERTS, 128))],
        out_shape=[jax.ShapeDtypeStruct((T, D), F32),
                   jax.ShapeDtypeStruct((T, D), F32),
                   jax.ShapeDtypeStruct((8, T), F32),
                   jax.ShapeDtypeStruct((N_EXPERTS, 128), F32)],
        scratch_shapes=[pltpu.VMEM((TILE_A, D), F32), pltpu.VMEM((N_EXPERTS, 128), F32)],
        compiler_params=_cparams(),
        name="attn_mixer",
    )(q, kv, kv, x2, o_s, *consts)


def _attn_sample_body(q_ref, kvn_ref, ck_ref, cv_ref, bm_ref, vc_ref, vn_ref, sink_ref,
                      o_ref, nk_ref, nv_ref):
    bm = bm_ref[...]
    valid_c = vc_ref[...] > 0.0
    valid_n = vn_ref[...] > 0.0
    sink = sink_ref[...]
    hist = WINDOW - DEC_SEQ

    def one(b, carry):
        q8 = q_ref[b]
        kvn = kvn_ref[b]
        qm = jnp.concatenate(
            [q8[:, r * KVW:(r + 1) * KVW] for r in range(GQA) for _ in range(N_KV)], axis=0)
        qm = (qm * bm).astype(BF16)
        s_c = jnp.where(valid_c, _dot_nt(qm, ck_ref[b].astype(BF16)), MASK_VALUE)
        s_n = jnp.where(valid_n, _dot_nt(qm, kvn[:, :KVW].astype(BF16)), MASK_VALUE)
        m = jnp.maximum(jnp.maximum(jnp.max(s_c, axis=-1, keepdims=True),
                                    jnp.max(s_n, axis=-1, keepdims=True)), sink)
        p_c = jnp.exp(s_c - m)
        p_n = jnp.exp(s_n - m)
        den = (jnp.sum(p_c, axis=-1, keepdims=True) + jnp.sum(p_n, axis=-1, keepdims=True)
               + jnp.exp(sink - m))
        o = (_dot(p_c.astype(BF16), cv_ref[b].astype(BF16))
             + _dot(p_n.astype(BF16), kvn[:, KVW:].astype(BF16))) * (bm * (1.0 / den))
        outs = []
        for r in range(GQA):
            acc = o[r * 32:r * 32 + 8]
            for g in range(1, N_KV):
                acc = acc + o[r * 32 + g * 8:r * 32 + g * 8 + 8]
            outs.append(acc)
        o_ref[b] = jnp.concatenate(outs, axis=1)
        nk_ref[b, 0:hist, :] = ck_ref[b, DEC_SEQ:WINDOW, :]
        nk_ref[b, hist:WINDOW, :] = kvn_ref[b, 0:DEC_SEQ, 0:KVW]
        nv_ref[b, 0:hist, :] = cv_ref[b, DEC_SEQ:WINDOW, :]
        nv_ref[b, hist:WINDOW, :] = kvn_ref[b, 0:DEC_SEQ, KVW:2 * KVW]
        return carry

    lax.fori_loop(0, SAMPLE_BB, one, 0, unroll=2)


def _attn_sample(q8, kvn16, ck, cv, bm, vc, vn, sink_col8):
    bb = SAMPLE_BB
    blk = lambda i: (i, 0, 0)
    consts = [bm, vc, vn, sink_col8]
    return pl.pallas_call(
        _attn_sample_body,
        grid=(DEC_BATCH // bb,),
        in_specs=[pl.BlockSpec((bb, 8, D), blk),
                  pl.BlockSpec((bb, NKN, 2 * KVW), blk),
                  pl.BlockSpec((bb, WINDOW, KVW), blk),
                  pl.BlockSpec((bb, WINDOW, KVW), blk)] + [_const_spec(c.shape) for c in consts],
        out_specs=[pl.BlockSpec((bb, 8, D), blk),
                   pl.BlockSpec((bb, WINDOW, KVW), blk),
                   pl.BlockSpec((bb, WINDOW, KVW), blk)],
        out_shape=[jax.ShapeDtypeStruct((DEC_BATCH, 8, D), F32),
                   jax.ShapeDtypeStruct((DEC_BATCH, WINDOW, KVW), F32),
                   jax.ShapeDtypeStruct((DEC_BATCH, WINDOW, KVW), F32)],
        compiler_params=_cparams(),
        name="attn_sample",
    )(q8, kvn16, ck, cv, *consts)


def _router_weights(w_group, b_group, w_router, b_router):
    wt = jnp.zeros((N_LOGIT_ROWS, D), F32)
    wt = wt.at[0:N_GROUPS].set(w_group.T)
    wt = wt.at[8:8 + N_EXPERTS].set(jnp.transpose(w_router, (0, 2, 1)).reshape(N_EXPERTS, D))
    hi = wt.astype(BF16)
    lo = (wt - hi.astype(F32)).astype(BF16)
    a = jnp.concatenate([hi, lo], axis=0)
    bias = jnp.zeros((N_LOGIT_ROWS,), F32)
    bias = bias.at[0:N_GROUPS].set(b_group)
    bias = bias.at[N_GROUPS:8].set(MASK_VALUE)
    bias = bias.at[8:8 + N_EXPERTS].set(b_router.reshape(N_EXPERTS))
    return a, jnp.broadcast_to(bias[:, None], (N_LOGIT_ROWS, 128))


def _routing_tables(cnt):
    counts = cnt[:, 0].astype(jnp.int32)
    padded = ((counts + TILE_M - 1) // TILE_M) * TILE_M
    ends = jnp.cumsum(padded)
    offs = ends - padded
    tile_start = jnp.arange(N_TILES, dtype=jnp.int32) * TILE_M
    n_used = (ends[-1] // TILE_M).astype(jnp.int32)
    tile_e = jnp.sum(tile_start[:, None] >= ends[None, :], axis=1).astype(jnp.int32)
    last_e = tile_e[jnp.maximum(n_used - 1, 0)]
    tile_e = jnp.where(tile_start < ends[-1], tile_e, last_e)
    cstart = jnp.concatenate([jnp.maximum(ends - TILE_M, 0), ends[-1:]])
    cnum = jnp.concatenate([jnp.where(padded > 0, TILE_M // 8, 0), (P_ROWS - ends[-1:]) // 8])
    pad = jnp.concatenate([cstart, cnum]).astype(jnp.int32)
    return offs.astype(jnp.int32), pad, tile_e, n_used.reshape(1)


def _moe_layer(layer, h, meta, cnt, w1, w3, w2):
    offs, pad, tile_e, n_used = _routing_tables(cnt)
    pos = _positions(offs, meta).reshape(N_ASSIGN)
    ys = _expert_ffn(layer, tile_e, n_used, _inverse_positions(pos, pad), h, w1, w3, w2)
    return pos, ys


def _heads_r_major(w):
    return jnp.transpose(w.reshape(N_KV, GQA, HEAD_DIM, -1), (1, 0, 2, 3)).reshape(D, -1)


def _block_mask(rows_per_block, n_rep):
    n = n_rep * N_KV * rows_per_block
    r = (jnp.arange(n)[:, None] // rows_per_block) % N_KV
    c = jnp.arange(KVW)[None, :] // HEAD_DIM
    return (r == c).astype(F32)


def kernel(x_prompt, x_sample, state_conv, cache_k, cache_v, norm_mix, norm_ffn, conv_w_in, conv_b_in, conv_w_dw, conv_b_dw, conv_ln_g, conv_ln_b, conv_w_out, conv_b_out, norm_kv, w_kv, b_kv, attn_w_q, attn_b_q, attn_sinks, attn_w_o, attn_b_o, moe_w_group, moe_b_group, moe_w_router, moe_b_router, moe_w1, moe_w3, moe_w2, norm_final):
    r2 = lambda v: v.reshape(1, -1)
    n_tri = max(TILE_SEQ, TILE_A)
    u_tri = jnp.triu(jnp.ones((n_tri, n_tri), BF16), 1)
    ar0, rb0 = _router_weights(moe_w_group[0], moe_b_group[0], moe_w_router[0], moe_b_router[0])
    ar1, rb1 = _router_weights(moe_w_group[1], moe_b_group[1], moe_w_router[1], moe_b_router[1])

    win = conv_w_in[0].astype(BF16)
    wout = conv_w_out[0].astype(BF16)
    wdw = conv_w_dw[0]
    xs2 = x_sample.reshape(TS, D)
    u_s = _glu_sample(xs2, r2(norm_mix[0]), win, r2(conv_b_in[0]))
    tt = jnp.arange(DEC_SEQ)[:, None]
    jh = jnp.arange(CONV_HIST)[None, :]
    wh = jnp.where((jh >= tt)[..., None], wdw[jnp.clip(jh - tt, 0, CONV_W - 1)], 0.0)
    ju = jnp.arange(DEC_SEQ)[None, :]
    wu = jnp.where((ju <= tt)[..., None], wdw[jnp.clip(CONV_HIST - tt + ju, 0, CONV_W - 1)], 0.0)
    y_s, state_s = _dwconv_sample(u_s.reshape(DEC_BATCH, DEC_SEQ, D), state_conv, wh, wu)
    wdw8 = jnp.broadcast_to(wdw[:, None, :], (CONV_W, 8, D))
    x1, h, meta, cnt, state_p = _conv_mixer(
        x_prompt.reshape(TP, D), xs2, y_s.reshape(TS, D), r2(norm_mix[0]), win, r2(conv_b_in[0]), wdw8,
        r2(conv_b_dw[0]), r2(conv_ln_g[0]), r2(conv_ln_b[0]), wout, r2(conv_b_out[0]), r2(norm_ffn[0]),
        ar0, rb0, u_tri[:TILE_SEQ, :TILE_SEQ])

    pos, ys = _moe_layer(0, h, meta, cnt, moe_w1, moe_w3, moe_w2)
    scale = HEAD_DIM ** -0.5
    wq4 = attn_w_q[0].reshape(D, N_KV, GQA, HEAD_DIM)
    wq = (jnp.transpose(wq4, (0, 2, 1, 3)).reshape(D, D) * scale).astype(BF16)
    bq = r2(_heads_r_major(attn_b_q[0][:, None])[:, 0] * scale)
    wo = _heads_r_major(attn_w_o[0]).astype(BF16)
    x2, kv, q = _combine_kvq(pos, ys, x1, meta, r2(norm_kv), w_kv.astype(BF16), r2(b_kv),
                             r2(norm_mix[1]), wq, bq)

    sinks = attn_sinks[0].astype(F32).reshape(N_KV, GQA).T
    sink_col = jnp.repeat(sinks.reshape(-1), WINDOW)[:, None]
    sink_col8 = jnp.repeat(sinks.reshape(-1), 8)[:, None]
    qi = (jnp.arange(GQA * N_KV * 8) % 8)[:, None]
    vc = ((qi < DEC_SEQ) & (jnp.arange(WINDOW)[None, :] > qi)).astype(F32)
    vn = ((qi < DEC_SEQ) & (jnp.arange(NKN)[None, :] <= qi)).astype(F32)
    q8 = jnp.pad(q[TP:].reshape(DEC_BATCH, DEC_SEQ, D), ((0, 0), (0, 8 - DEC_SEQ), (0, 0)))
    kvn = jnp.pad(kv[TP:].reshape(DEC_BATCH, DEC_SEQ, 2 * KVW), ((0, 0), (0, NKN - DEC_SEQ), (0, 0)))
    o_s8, nk_s, nv_s = _attn_sample(q8, kvn, cache_k.reshape(DEC_BATCH, WINDOW, KVW),
                                    cache_v.reshape(DEC_BATCH, WINDOW, KVW),
                                    _block_mask(8, GQA), vc, vn, sink_col8)
    o_s = o_s8[:, :DEC_SEQ].reshape(TS, D)
    pq = (jnp.arange(N_KV * WINDOW) % WINDOW)[:, None]
    pk = jnp.arange(2 * WINDOW)[None, :]
    dist = pq + WINDOW - pk
    band_rest = (dist >= 0) & (dist < WINDOW)
    band = jnp.stack([band_rest & (pk >= WINDOW), band_rest]).astype(F32)
    x3, h, meta, cnt = _attn_mixer(q, kv, x2, o_s, _block_mask(WINDOW, 1), band, sink_col, wo,
                                   r2(attn_b_o[0]), r2(norm_ffn[1]), ar1, rb1, u_tri[:TILE_A, :TILE_A])

    pos, ys = _moe_layer(1, h, meta, cnt, moe_w1, moe_w3, moe_w2)
    y_p, y_s2 = _combine_final(pos, ys, x3, meta, r2(norm_final))

    kvp = jnp.stack([kv[(b + 1) * SEQ - WINDOW:(b + 1) * SEQ] for b in range(BATCH)])
    new_k_p = kvp[..., :KVW].reshape(BATCH, WINDOW, N_KV, HEAD_DIM)
    new_v_p = kvp[..., KVW:].reshape(BATCH, WINDOW, N_KV, HEAD_DIM)
    return (y_p.reshape(BATCH, SEQ, D), y_s2.reshape(DEC_BATCH, DEC_SEQ, D),
            state_p[None], state_s, new_k_p, new_v_p,
            nk_s.reshape(DEC_BATCH, WINDOW, N_KV, HEAD_DIM), nv_s.reshape(DEC_BATCH, WINDOW, N_KV, HEAD_DIM))
```

```python
import jax
import jax.numpy as jnp
from jax import lax
from jax.experimental import pallas as pl
from jax.experimental.pallas import tpu as pltpu

D = 1024
BATCH = 8
SEQ = 2048
DEC_BATCH = 128
DEC_SEQ = 4
CONV_W = 31
CONV_HIST = CONV_W - 1
HEAD_DIM = 64
N_HEADS = 16
N_KV = 4
GQA = 4
KVW = N_KV * HEAD_DIM
WINDOW = 128
N_GROUPS = 4
EPG = 8
N_EXPERTS = 32
D_EXPERT = 512
RMS_EPS = 1e-5
LN_EPS = 1e-5
MASK_VALUE = -1e30

TP = BATCH * SEQ
TS = DEC_BATCH * DEC_SEQ
T = TP + TS
N_ASSIGN = 2 * T

TILE_SEQ = 512
TILE_M = 512
N_TILES = (N_ASSIGN + N_EXPERTS * (TILE_M - 1)) // TILE_M + 1
P_ROWS = N_TILES * TILE_M
TILE_C = 256
N_LOGIT_ROWS = 48
SAMPLE_BB = 16
NKN = 16
VMEM_LIMIT = 60 * 1024 * 1024

F32 = jnp.float32
BF16 = jnp.bfloat16


def _cparams(n_axes=1):
    return pltpu.CompilerParams(dimension_semantics=("arbitrary",) * n_axes,
                                vmem_limit_bytes=VMEM_LIMIT)


def _const_spec(shape):
    nd = len(shape)
    return pl.BlockSpec(shape, lambda *_: (0,) * nd)


def _rms(x, g):
    return x * lax.rsqrt(jnp.mean(x * x, axis=-1, keepdims=True) + RMS_EPS) * g


def _dot(a, b):
    return jnp.dot(a, b, preferred_element_type=F32)


def _dot_nt(a, b):
    return lax.dot_general(a, b, (((1,), (1,)), ((), ())), preferred_element_type=F32)


def _moe_prologue(x, g_ffn, a_ref, rbias_ref, u_ref, cnt_ref):
    n = x.shape[0]
    h = _rms(x, g_ffn)
    h_hi = h.astype(BF16)
    h_lo = (h - h_hi.astype(F32)).astype(BF16)
    a = a_ref[...]
    l1 = _dot_nt(a, h_hi)
    l2 = _dot_nt(a[:N_LOGIT_ROWS], h_lo)
    logits = l1[:N_LOGIT_ROWS] + l1[N_LOGIT_ROWS:] + l2 + rbias_ref[:, 0:1]
    iota8 = lax.broadcasted_iota(jnp.int32, (8, n), 0).astype(F32)
    gl = logits[0:8]
    gmax = jnp.max(gl, axis=0, keepdims=True)
    g_idx = jnp.min(jnp.where(gl == gmax, iota8, 8.0), axis=0, keepdims=True)
    g_w = 1.0 / jnp.sum(jnp.exp(gl - gmax), axis=0, keepdims=True)
    es = jnp.where(g_idx == 0.0, logits[8:16],
                   jnp.where(g_idx == 1.0, logits[16:24],
                             jnp.where(g_idx == 2.0, logits[24:32], logits[32:40])))
    v1 = jnp.max(es, axis=0, keepdims=True)
    i1 = jnp.min(jnp.where(es == v1, iota8, 8.0), axis=0, keepdims=True)
    es2 = jnp.where(iota8 == i1, -jnp.inf, es)
    v2 = jnp.max(es2, axis=0, keepdims=True)
    i2 = jnp.min(jnp.where(es2 == v2, iota8, 8.0), axis=0, keepdims=True)
    e2x = jnp.exp(v2 - v1)
    w1 = g_w / (1.0 + e2x)
    w2 = g_w * e2x / (1.0 + e2x)
    ex1 = g_idx * EPG + i1
    ex2 = g_idx * EPG + i2
    iota32 = lax.broadcasted_iota(jnp.int32, (N_EXPERTS, n), 0).astype(F32)
    oh1 = jnp.where(iota32 == ex1, 1.0, 0.0)
    oh2 = jnp.where(iota32 == ex2, 1.0, 0.0)
    onehot = oh1 + oh2
    before = _dot(onehot.astype(BF16), u_ref[...]) + cnt_ref[:, 0:1]
    rank1 = jnp.sum(oh1 * before, axis=0, keepdims=True)
    rank2 = jnp.sum(oh2 * before, axis=0, keepdims=True)
    cnt_ref[...] = cnt_ref[...] + jnp.sum(onehot, axis=1, keepdims=True)
    rows = (ex1, ex2, w1, w2, rank1, rank2)
    meta = jnp.zeros((8, n), F32)
    for r, val in enumerate(rows):
        meta = jnp.where(iota8 == r, val, meta)
    return h, meta


def _ln_silu(y, g, b):
    mu = jnp.mean(y, axis=-1, keepdims=True)
    yc = y - mu
    var = jnp.mean(yc * yc, axis=-1, keepdims=True)
    z = yc * lax.rsqrt(var + LN_EPS) * g + b
    return z * jax.nn.sigmoid(z)


CONV_CH = 32
NP_CONV = TP // TILE_SEQ
NSQ = SEQ // TILE_SEQ


def _conv_mixer_body(xp_ref, xs_ref, ys_ref, gmix_ref, win_ref, bin_ref, wdw_ref, bdw_ref,
                     lng_ref, lnb_ref, wout_ref, bout_ref, gffn_ref, a_ref, rbias_ref, u_ref,
                     x1_ref, h_ref, meta_ref, cnt_out_ref, state_ref,
                     ubuf, ush, ybuf, cnt_scr):
    i = pl.program_id(0)
    is_prompt = i < NP_CONV
    s = i % NSQ

    @pl.when(i == 0)
    def _():
        cnt_scr[...] = jnp.zeros_like(cnt_scr)

    @pl.when(is_prompt)
    def _():
        @pl.when(s == 0)
        def _():
            ubuf[0:32, :] = jnp.zeros((32, D), F32)

        hn = _rms(xp_ref[...], gmix_ref[...])
        u2 = _dot(hn.astype(BF16), win_ref[...]) + bin_ref[...]
        ubuf[32:32 + TILE_SEQ, :] = u2[:, :D] * jax.nn.sigmoid(u2[:, D:])

        for sh in range(1, 8):
            ush[sh - 1] = ubuf[sh:sh + TILE_SEQ + 24, :]

        ngrp = CONV_CH // 8

        def chunk(c, carry):
            r0 = pl.multiple_of(c * CONV_CH, CONV_CH)
            for lt in range(D // 128):
                lanes = slice(lt * 128, (lt + 1) * 128)
                acc = [None] * ngrp
                for sh in range(8):
                    a8s = [a8 for a8 in range(5) if 0 <= 8 * a8 + sh - 2 < CONV_W]
                    win = {}
                    for gi in range(a8s[0], a8s[-1] + ngrp):
                        start = pl.multiple_of(r0 + 8 * gi, 8)
                        if sh == 0:
                            win[gi] = ubuf[pl.ds(start, 8), lanes]
                        else:
                            win[gi] = ush[sh - 1, pl.ds(start, 8), lanes]
                    for a8 in a8s:
                        w = wdw_ref[8 * a8 + sh - 2, :, lanes]
                        for gq in range(ngrp):
                            term = w * win[a8 + gq]
                            acc[gq] = term if acc[gq] is None else acc[gq] + term
                for gq in range(ngrp):
                    ybuf[pl.ds(pl.multiple_of(r0 + 8 * gq, 8), 8), lanes] = acc[gq]
            return carry

        lax.fori_loop(0, TILE_SEQ // CONV_CH, chunk, 0)

        @pl.when(s == NSQ - 1)
        def _():
            state_ref[0] = ubuf[TILE_SEQ + 2:TILE_SEQ + 32, :]

        ubuf[0:32, :] = ubuf[TILE_SEQ:TILE_SEQ + 32, :]

    @pl.when(jnp.logical_not(is_prompt))
    def _():
        ybuf[...] = ys_ref[...]

    x = jnp.where(is_prompt, xp_ref[...], xs_ref[...])
    act = _ln_silu(ybuf[...] + bdw_ref[...], lng_ref[...], lnb_ref[...])
    x1 = x + _dot(act.astype(BF16), wout_ref[...]) + bout_ref[...]
    x1_ref[...] = x1
    h, meta = _moe_prologue(x1, gffn_ref[...], a_ref, rbias_ref, u_ref, cnt_scr)
    h_ref[...] = h
    meta_ref[...] = meta
    cnt_out_ref[...] = cnt_scr[...]


def _conv_mixer(xp, xs, ys, gmix, win, b_in, wdw, bdw, lng, lnb, wout, bout, gffn, a, rbias, u):
    consts = [gmix, win, b_in, wdw, bdw, lng, lnb, wout, bout, gffn, a, rbias, u]
    prow = lambda i: (jnp.minimum(i, NP_CONV - 1), 0)
    srow = lambda i: (jnp.maximum(i - NP_CONV, 0), 0)
    row = lambda i: (i, 0)
    return pl.pallas_call(
        _conv_mixer_body,
        grid=(T // TILE_SEQ,),
        in_specs=[pl.BlockSpec((TILE_SEQ, D), prow),
                  pl.BlockSpec((TILE_SEQ, D), srow),
                  pl.BlockSpec((TILE_SEQ, D), srow)] + [_const_spec(c.shape) for c in consts],
        out_specs=[
            pl.BlockSpec((TILE_SEQ, D), row),
            pl.BlockSpec((TILE_SEQ, D), row),
            pl.BlockSpec((8, TILE_SEQ), lambda i: (0, i)),
            _const_spec((N_EXPERTS, 128)),
            pl.BlockSpec((1, CONV_HIST, D), lambda i: (jnp.minimum(i // NSQ, BATCH - 1), 0, 0)),
        ],
        out_shape=[
            jax.ShapeDtypeStruct((T, D), F32),
            jax.ShapeDtypeStruct((T, D), F32),
            jax.ShapeDtypeStruct((8, T), F32),
            jax.ShapeDtypeStruct((N_EXPERTS, 128), F32),
            jax.ShapeDtypeStruct((BATCH, CONV_HIST, D), F32),
        ],
        scratch_shapes=[
            pltpu.VMEM((TILE_SEQ + 32, D), F32),
            pltpu.VMEM((7, TILE_SEQ + 24, D), F32),
            pltpu.VMEM((TILE_SEQ, D), F32),
            pltpu.VMEM((N_EXPERTS, 128), F32),
        ],
        compiler_params=_cparams(),
        name="conv_mixer",
    )(xp, xs, ys, *consts)


def _glu_sample_body(x_ref, gmix_ref, win_ref, bin_ref, u_ref):
    hn = _rms(x_ref[...], gmix_ref[...])
    u2 = _dot(hn.astype(BF16), win_ref[...]) + bin_ref[...]
    u_ref[...] = u2[:, :D] * jax.nn.sigmoid(u2[:, D:])


def _glu_sample(x, gmix, win, b_in):
    args = [x, gmix, win, b_in]
    return pl.pallas_call(
        _glu_sample_body,
        grid=(1,),
        in_specs=[_const_spec(a.shape) for a in args],
        out_specs=_const_spec((TS, D)),
        out_shape=jax.ShapeDtypeStruct((TS, D), F32),
        compiler_params=_cparams(),
        name="glu_sample",
    )(*args)


def _dwconv_sample_body(u_ref, st_ref, wh_ref, wu_ref, y_ref, nst_ref):
    st = st_ref[0]
    u = u_ref[...]
    for t in range(DEC_SEQ):
        y_ref[:, t:t + 1, :] = (jnp.sum(st * wh_ref[t][None], axis=1, keepdims=True)
                                + jnp.sum(u * wu_ref[t][None], axis=1, keepdims=True))
    nst_ref[0, :, 0:CONV_HIST - DEC_SEQ, :] = st_ref[0, :, DEC_SEQ:CONV_HIST, :]
    nst_ref[0, :, CONV_HIST - DEC_SEQ:CONV_HIST, :] = u


def _dwconv_sample(u3, state, wh, wu):
    bb = SAMPLE_BB
    blk = lambda i: (i, 0, 0)
    sblk = lambda i: (0, i, 0, 0)
    return pl.pallas_call(
        _dwconv_sample_body,
        grid=(DEC_BATCH // bb,),
        in_specs=[
            pl.BlockSpec((bb, DEC_SEQ, D), blk),
            pl.BlockSpec((1, bb, CONV_HIST, D), sblk),
            _const_spec(wh.shape), _const_spec(wu.shape),
        ],
        out_specs=[pl.BlockSpec((bb, DEC_SEQ, D), blk), pl.BlockSpec((1, bb, CONV_HIST, D), sblk)],
        out_shape=[
            jax.ShapeDtypeStruct((DEC_BATCH, DEC_SEQ, D), F32),
            jax.ShapeDtypeStruct((1, DEC_BATCH, CONV_HIST, D), F32),
        ],
        compiler_params=_cparams(),
        name="dwconv_sample",
    )(u3, state, wh, wu)


def _positions_body(offs_ref, meta_ref, pos_ref):
    ex = meta_ref[0:2, :]
    acc = meta_ref[4:6, :]
    for e in range(N_EXPERTS):
        acc = acc + jnp.where(ex == float(e), offs_ref[e].astype(F32), 0.0)
    pos_ref[...] = acc.astype(jnp.int32)


def _positions(offs, meta):
    return pl.pallas_call(
        _positions_body,
        grid_spec=pltpu.PrefetchScalarGridSpec(
            num_scalar_prefetch=1,
            grid=(1,),
            in_specs=[pl.BlockSpec((8, T), lambda i, o: (0, 0))],
            out_specs=pl.BlockSpec((2, T), lambda i, o: (0, 0)),
        ),
        out_shape=jax.ShapeDtypeStruct((2, T), jnp.int32),
        compiler_params=_cparams(),
        name="positions",
    )(offs, meta)


def _inverse_body(pos_ref, pad_ref, src_ref):
    def clear(e, carry):
        start = pad_ref[e]

        def chunk(c, cc):
            for j in range(8):
                p = start + c * 8 + j
                src_ref[p] = p & (TP - 1)
            return cc

        return lax.fori_loop(0, pad_ref[N_EXPERTS + 1 + e], chunk, carry)

    lax.fori_loop(0, N_EXPERTS + 1, clear, 0)

    def fill(c, carry):
        for j in range(8):
            t = c * 8 + j
            src_ref[pos_ref[t]] = t
            src_ref[pos_ref[T + t]] = t
        return carry

    lax.fori_loop(0, T // 8, fill, 0)


def _inverse_positions(pos, pad):
    return pl.pallas_call(
        _inverse_body,
        grid_spec=pltpu.PrefetchScalarGridSpec(
            num_scalar_prefetch=2,
            grid=(1,),
            in_specs=[],
            out_specs=pl.BlockSpec(memory_space=pltpu.SMEM),
        ),
        out_shape=jax.ShapeDtypeStruct((P_ROWS,), jnp.int32),
        compiler_params=_cparams(),
        name="inverse_positions",
    )(pos, pad)


def _expert_gather_wait(h_hbm, sem, slot):
    src = h_hbm.at[pl.ds(0, TILE_M), :]
    pltpu.make_async_copy(src, src, sem.at[slot]).wait()


def _expert_body(te_ref, nu_ref, src_ref, h_hbm, w1_ref, w3_ref, w2_ref, y_ref,
                 xbuf_a, xbuf_b, w1b, w3b, w2b, sem):
    i = pl.program_id(0)
    nu = nu_ref[0]
    live = i < nu
    prev = te_ref[jnp.maximum(i - 1, 0)]
    fresh = (i == 0) | (te_ref[i] != prev)
    xbufs = (xbuf_a, xbuf_b)

    @pl.when(i == 0)
    def _():
        def issue(c, carry):
            for j in range(8):
                t = c * 8 + j
                pltpu.make_async_copy(h_hbm.at[pl.ds(src_ref[t], 1), :],
                                      xbuf_a.at[c, pl.ds(j, 1), :], sem.at[0]).start(priority=j % 2)
            return carry

        lax.fori_loop(0, TILE_M // 8, issue, 0)

    @pl.when(live & fresh)
    def _():
        w1b[...] = w1_ref[0, 0].astype(BF16)
        w3b[...] = w3_ref[0, 0].astype(BF16)
        w2b[...] = w2_ref[0, 0].astype(BF16)

    for slot in range(2):
        @pl.when(live & (i % 2 == slot))
        def _(slot=slot):
            _expert_gather_wait(h_hbm, sem, slot)
            base = jnp.minimum(i + 1, nu - 1) * TILE_M
            for t in range(TILE_M):
                pltpu.make_async_copy(h_hbm.at[pl.ds(src_ref[base + t], 1), :],
                                      xbufs[1 - slot].at[t // 8, pl.ds(t % 8, 1), :],
                                      sem.at[1 - slot]).start(priority=t % 2)
            xb = xbufs[slot][...].reshape(TILE_M, D).astype(BF16)
            a = _dot(xb, w1b[...])
            g = _dot(xb, w3b[...])
            act = a * jax.nn.sigmoid(a) * g
            y_ref[...] = _dot(act.astype(BF16), w2b[...])

            @pl.when(i == nu - 1)
            def _():
                _expert_gather_wait(h_hbm, sem, 1 - slot)

    @pl.when(jnp.logical_not(live))
    def _():
        y_ref[...] = jnp.zeros_like(y_ref)


def _expert_ffn(layer, tile_e, n_used, src, h, w1, w3, w2):
    wsel = lambda i, te, nu, sr: (layer, te[i], 0, 0)
    return pl.pallas_call(
        _expert_body,
        grid_spec=pltpu.PrefetchScalarGridSpec(
            num_scalar_prefetch=3,
            grid=(N_TILES,),
            in_specs=[
                pl.BlockSpec(memory_space=pl.ANY),
                pl.BlockSpec((1, 1, D, D_EXPERT), wsel),
                pl.BlockSpec((1, 1, D, D_EXPERT), wsel),
                pl.BlockSpec((1, 1, D_EXPERT, D), wsel),
            ],
            out_specs=pl.BlockSpec((TILE_M, D), lambda i, te, nu, sr: (i, 0)),
            scratch_shapes=[
                pltpu.VMEM((TILE_M // 8, 8, D), F32),
                pltpu.VMEM((TILE_M // 8, 8, D), F32),
                pltpu.VMEM((D, D_EXPERT), BF16),
                pltpu.VMEM((D, D_EXPERT), BF16),
                pltpu.VMEM((D_EXPERT, D), BF16),
                pltpu.SemaphoreType.DMA((2,)),
            ],
        ),
        out_shape=jax.ShapeDtypeStruct((P_ROWS, D), F32),
        compiler_params=_cparams(),
        name="expert_ffn",
    )(tile_e, n_used, src, h, w1, w3, w2)


def _gather_wait(y_hbm, sem, slot):
    src = y_hbm.at[pl.ds(0, TILE_C), :]
    for k in range(2):
        pltpu.make_async_copy(src, src, sem.at[slot]).wait()


def _combine_steps(pos_ref, y_hbm, x_ref, meta_ref, ybufs, sem, finish):
    i = pl.program_id(0)
    n = pl.num_programs(0)

    @pl.when(i == 0)
    def _():
        def issue(c, carry):
            for j in range(8):
                for k in range(2):
                    pltpu.make_async_copy(y_hbm.at[pl.ds(pos_ref[k * T + c * 8 + j], 1), :],
                                          ybufs[0].at[k, c, pl.ds(j, 1), :], sem.at[0]).start(priority=k)
            return carry

        lax.fori_loop(0, TILE_C // 8, issue, 0)

    for slot in range(2):
        @pl.when(i % 2 == slot)
        def _(slot=slot):
            _gather_wait(y_hbm, sem, slot)
            base = jnp.minimum(i + 1, n - 1) * TILE_C
            for t in range(TILE_C):
                for k in range(2):
                    pltpu.make_async_copy(y_hbm.at[pl.ds(pos_ref[k * T + base + t], 1), :],
                                          ybufs[1 - slot].at[k, t // 8, pl.ds(t % 8, 1), :],
                                          sem.at[1 - slot]).start(priority=k)
            mt = jnp.concatenate([meta_ref[...], jnp.zeros((120, TILE_C), F32)], axis=0).T
            y0 = ybufs[slot][0].reshape(TILE_C, D)
            y1 = ybufs[slot][1].reshape(TILE_C, D)
            finish(x_ref[...] + mt[:, 2:3] * y0 + mt[:, 3:4] * y1)

            @pl.when(i == n - 1)
            def _():
                _gather_wait(y_hbm, sem, 1 - slot)


def _combine_scratch():
    buf = pltpu.VMEM((2, TILE_C // 8, 8, D), F32)
    return [buf, buf, pltpu.SemaphoreType.DMA((2,))]


def _combine_kvq_body(pos_ref, y_hbm, x_ref, meta_ref, gkv_ref, wkv_ref, bkv_ref, gq_ref, wq_ref, bq_ref,
                      x2_ref, kv_ref, q_ref, ybuf_a, ybuf_b, sem):
    def finish(x2):
        x2_ref[...] = x2
        kv_ref[...] = _dot(_rms(x2, gkv_ref[...]).astype(BF16), wkv_ref[...]) + bkv_ref[...]
        q_ref[...] = _dot(_rms(x2, gq_ref[...]).astype(BF16), wq_ref[...]) + bq_ref[...]

    _combine_steps(pos_ref, y_hbm, x_ref, meta_ref, (ybuf_a, ybuf_b), sem, finish)


def _combine_kvq(pos, ys, x1, meta, gkv, wkv, bkv, gq, wq, bq):
    row = lambda i, p: (i, 0)
    consts = [gkv, wkv, bkv, gq, wq, bq]
    return pl.pallas_call(
        _combine_kvq_body,
        grid_spec=pltpu.PrefetchScalarGridSpec(
            num_scalar_prefetch=1,
            grid=(T // TILE_C,),
            in_specs=[pl.BlockSpec(memory_space=pl.ANY),
                      pl.BlockSpec((TILE_C, D), row),
                      pl.BlockSpec((8, TILE_C), lambda i, p: (0, i))]
                     + [pl.BlockSpec(c.shape, lambda i, p, nd=c.ndim: (0,) * nd) for c in consts],
            out_specs=[pl.BlockSpec((TILE_C, D), row),
                       pl.BlockSpec((TILE_C, 2 * KVW), row),
                       pl.BlockSpec((TILE_C, D), row)],
            scratch_shapes=_combine_scratch(),
        ),
        out_shape=[jax.ShapeDtypeStruct((T, D), F32),
                   jax.ShapeDtypeStruct((T, 2 * KVW), F32),
                   jax.ShapeDtypeStruct((T, D), F32)],
        compiler_params=_cparams(),
        name="combine_kvq",
    )(pos, ys, x1, meta, *consts)


def _combine_final_body(pos_ref, y_hbm, x_ref, meta_ref, gf_ref, yp_ref, ys_ref, ybuf_a, ybuf_b, sem):
    i = pl.program_id(0)

    def finish(x4):
        out = _rms(x4, gf_ref[...])

        @pl.when(i < TP // TILE_C)
        def _():
            yp_ref[...] = out

        @pl.when(i >= TP // TILE_C)
        def _():
            ys_ref[...] = out

    _combine_steps(pos_ref, y_hbm, x_ref, meta_ref, (ybuf_a, ybuf_b), sem, finish)


def _combine_final(pos, ys, x3, meta, gf):
    row = lambda i, p: (i, 0)
    npt = TP // TILE_C
    return pl.pallas_call(
        _combine_final_body,
        grid_spec=pltpu.PrefetchScalarGridSpec(
            num_scalar_prefetch=1,
            grid=(T // TILE_C,),
            in_specs=[pl.BlockSpec(memory_space=pl.ANY),
                      pl.BlockSpec((TILE_C, D), row),
                      pl.BlockSpec((8, TILE_C), lambda i, p: (0, i)),
                      pl.BlockSpec((1, D), lambda i, p: (0, 0))],
            out_specs=[pl.BlockSpec((TILE_C, D), lambda i, p: (jnp.minimum(i, npt - 1), 0)),
                       pl.BlockSpec((TILE_C, D), lambda i, p: (jnp.maximum(i - npt, 0), 0))],
            scratch_shapes=_combine_scratch(),
        ),
        out_shape=[jax.ShapeDtypeStruct((TP, D), F32), jax.ShapeDtypeStruct((TS, D), F32)],
        compiler_params=_cparams(),
        name="combine_final",
    )(pos, ys, x3, meta, gf)


ATT_SUB = 4
ATT_GROUP = 8
TILE_A = ATT_SUB * WINDOW
NB_ATT = SEQ // TILE_A
NP_ATT = TP // TILE_A


def _attn_mixer_body(q_ref, kvc_ref, kvp_ref, x_ref, os_ref, bm_ref, band_ref, sink_ref,
                     wo_ref, bo_ref, gffn_ref, a_ref, rbias_ref, u_ref,
                     x3_ref, h_ref, meta_ref, cnt_out_ref, obuf, cnt_scr):
    i = pl.program_id(0)
    is_prompt = i < NP_ATT
    j = i % NB_ATT

    @pl.when(i == 0)
    def _():
        cnt_scr[...] = jnp.zeros_like(cnt_scr)

    @pl.when(is_prompt)
    def _():
        kall = jnp.concatenate([kvp_ref[:, :KVW], kvc_ref[:, :KVW]], axis=0).astype(BF16)
        vall = jnp.concatenate([kvp_ref[:, KVW:], kvc_ref[:, KVW:]], axis=0).astype(BF16)
        bm = bm_ref[...]
        bm16 = bm.astype(BF16)
        kbs = [kall[sb * WINDOW:(sb + 2) * WINDOW] for sb in range(ATT_SUB)]
        vbs = [vall[sb * WINDOW:(sb + 2) * WINDOW] for sb in range(ATT_SUB)]
        qbs = [q_ref[sb * WINDOW:(sb + 1) * WINDOW, :].astype(BF16) for sb in range(ATT_SUB)]
        valids = [band_ref[jnp.minimum(j, 1) if sb == 0 else 1] > 0.0 for sb in range(ATT_SUB)]
        all_units = [(sb, r) for sb in range(ATT_SUB) for r in range(GQA)]
        for u0 in range(0, len(all_units), ATT_GROUP):
            units = all_units[u0:u0 + ATT_GROUP]
            scores = []
            for sb, r in units:
                qm = jnp.concatenate([qbs[sb][:, r * KVW:(r + 1) * KVW]] * N_KV, axis=0) * bm16
                scores.append(_dot_nt(qm, kbs[sb]))
            probs, scales = [], []
            for (sb, r), s in zip(units, scores):
                s = jnp.where(valids[sb], s, MASK_VALUE)
                sink = sink_ref[r * N_KV * WINDOW:(r + 1) * N_KV * WINDOW]
                m = jnp.maximum(jnp.max(s, axis=-1, keepdims=True), sink)
                p = jnp.exp(s - m)
                den = jnp.sum(p, axis=-1, keepdims=True) + jnp.exp(sink - m)
                probs.append(p.astype(BF16))
                scales.append(1.0 / den)
            outs = [_dot(p, vbs[sb]) for (sb, r), p in zip(units, probs)]
            for (sb, r), o, sc in zip(units, outs, scales):
                o = o * (bm * sc)
                acc = o[0:WINDOW]
                for g in range(1, N_KV):
                    acc = acc + o[g * WINDOW:(g + 1) * WINDOW]
                obuf[sb * WINDOW:(sb + 1) * WINDOW, r * KVW:(r + 1) * KVW] = acc

    @pl.when(jnp.logical_not(is_prompt))
    def _():
        obuf[...] = os_ref[...]

    x3 = x_ref[...] + _dot(obuf[...].astype(BF16), wo_ref[...]) + bo_ref[...]
    x3_ref[...] = x3
    h, meta = _moe_prologue(x3, gffn_ref[...], a_ref, rbias_ref, u_ref, cnt_scr)
    h_ref[...] = h
    meta_ref[...] = meta
    cnt_out_ref[...] = cnt_scr[...]


def _attn_mixer(q, kv, x2, o_s, bm, band, sink_col, wo, bo, gffn, a, rbias, u):
    consts = [bm, band, sink_col, wo, bo, gffn, a, rbias, u]
    prow = lambda i: (jnp.minimum(i, NP_ATT - 1), 0)

    def prev(i):
        ic = jnp.minimum(i, NP_ATT - 1)
        return (ATT_SUB * ic - jnp.where(ic % NB_ATT == 0, 0, 1), 0)

    row = lambda i: (i, 0)
    return pl.pallas_call(
        _attn_mixer_body,
        grid=(T // TILE_A,),
        in_specs=[pl.BlockSpec((TILE_A, D), prow),
                  pl.BlockSpec((TILE_A, 2 * KVW), prow),
                  pl.BlockSpec((WINDOW, 2 * KVW), prev),
                  pl.BlockSpec((TILE_A, D), row),
                  pl.BlockSpec((TILE_A, D), lambda i: (jnp.maximum(i - NP_ATT, 0), 0))]
                 + [_const_spec(c.shape) for c in consts],
        out_specs=[pl.BlockSpec((TILE_A, D), row),
                   pl.BlockSpec((TILE_A, D), row),
                   pl.BlockSpec((8, TILE_A), lambda i: (0, i)),
                   _const_spec((N_EXPERTS, 128))],
        out_shape=[jax.ShapeDtypeStruct((T, D), F32),
                   jax.ShapeDtypeStruct((T, D), F32),
                   jax.ShapeDtypeStruct((8, T), F32),
                   jax.ShapeDtypeStruct((N_EXPERTS, 128), F32)],
        scratch_shapes=[pltpu.VMEM((TILE_A, D), F32), pltpu.VMEM((N_EXPERTS, 128), F32)],
        compiler_params=_cparams(),
        name="attn_mixer",
    )(q, kv, kv, x2, o_s, *consts)


def _attn_sample_body(q_ref, kvn_ref, ck_ref, cv_ref, bm_ref, vc_ref, vn_ref, sink_ref,
                      o_ref, nk_ref, nv_ref):
    bm = bm_ref[...]
    valid_c = vc_ref[...] > 0.0
    valid_n = vn_ref[...] > 0.0
    sink = sink_ref[...]
    hist = WINDOW - DEC_SEQ

    def one(b, carry):
        q8 = q_ref[b]
        kvn = kvn_ref[b]
        qm = jnp.concatenate(
            [q8[:, r * KVW:(r + 1) * KVW] for r in range(GQA) for _ in range(N_KV)], axis=0)
        qm = (qm * bm).astype(BF16)
        s_c = jnp.where(valid_c, _dot_nt(qm, ck_ref[b].astype(BF16)), MASK_VALUE)
        s_n = jnp.where(valid_n, _dot_nt(qm, kvn[:, :KVW].astype(BF16)), MASK_VALUE)
        m = jnp.maximum(jnp.maximum(jnp.max(s_c, axis=-1, keepdims=True),
                                    jnp.max(s_n, axis=-1, keepdims=True)), sink)
        p_c = jnp.exp(s_c - m)
        p_n = jnp.exp(s_n - m)
        den = (jnp.sum(p_c, axis=-1, keepdims=True) + jnp.sum(p_n, axis=-1, keepdims=True)
               + jnp.exp(sink - m))
        o = (_dot(p_c.astype(BF16), cv_ref[b].astype(BF16))
             + _dot(p_n.astype(BF16), kvn[:, KVW:].astype(BF16))) * (bm * (1.0 / den))
        outs = []
        for r in range(GQA):
            acc = o[r * 32:r * 32 + 8]
            for g in range(1, N_KV):
                acc = acc + o[r * 32 + g * 8:r * 32 + g * 8 + 8]
            outs.append(acc)
        o_ref[b] = jnp.concatenate(outs, axis=1)
        nk_ref[b, 0:hist, :] = ck_ref[b, DEC_SEQ:WINDOW, :]
        nk_ref[b, hist:WINDOW, :] = kvn_ref[b, 0:DEC_SEQ, 0:KVW]
        nv_ref[b, 0:hist, :] = cv_ref[b, DEC_SEQ:WINDOW, :]
        nv_ref[b, hist:WINDOW, :] = kvn_ref[b, 0:DEC_SEQ, KVW:2 * KVW]
        return carry

    lax.fori_loop(0, SAMPLE_BB, one, 0, unroll=2)


def _attn_sample(q8, kvn16, ck, cv, bm, vc, vn, sink_col8):
    bb = SAMPLE_BB
    blk = lambda i: (i, 0, 0)
    consts = [bm, vc, vn, sink_col8]
    return pl.pallas_call(
        _attn_sample_body,
        grid=(DEC_BATCH // bb,),
        in_specs=[pl.BlockSpec((bb, 8, D), blk),
                  pl.BlockSpec((bb, NKN, 2 * KVW), blk),
                  pl.BlockSpec((bb, WINDOW, KVW), blk),
                  pl.BlockSpec((bb, WINDOW, KVW), blk)] + [_const_spec(c.shape) for c in consts],
        out_specs=[pl.BlockSpec((bb, 8, D), blk),
                   pl.BlockSpec((bb, WINDOW, KVW), blk),
                   pl.BlockSpec((bb, WINDOW, KVW), blk)],
        out_shape=[jax.ShapeDtypeStruct((DEC_BATCH, 8, D), F32),
                   jax.ShapeDtypeStruct((DEC_BATCH, WINDOW, KVW), F32),
                   jax.ShapeDtypeStruct((DEC_BATCH, WINDOW, KVW), F32)],
        compiler_params=_cparams(),
        name="attn_sample",
    )(q8, kvn16, ck, cv, *consts)


def _router_weights(w_group, b_group, w_router, b_router):
    wt = jnp.zeros((N_LOGIT_ROWS, D), F32)
    wt = wt.at[0:N_GROUPS].set(w_group.T)
    wt = wt.at[8:8 + N_EXPERTS].set(jnp.transpose(w_router, (0, 2, 1)).reshape(N_EXPERTS, D))
    hi = wt.astype(BF16)
    lo = (wt - hi.astype(F32)).astype(BF16)
    a = jnp.concatenate([hi, lo], axis=0)
    bias = jnp.zeros((N_LOGIT_ROWS,), F32)
    bias = bias.at[0:N_GROUPS].set(b_group)
    bias = bias.at[N_GROUPS:8].set(MASK_VALUE)
    bias = bias.at[8:8 + N_EXPERTS].set(b_router.reshape(N_EXPERTS))
    return a, jnp.broadcast_to(bias[:, None], (N_LOGIT_ROWS, 128))


def _routing_tables(cnt):
    counts = cnt[:, 0].astype(jnp.int32)
    padded = ((counts + TILE_M - 1) // TILE_M) * TILE_M
    ends = jnp.cumsum(padded)
    offs = ends - padded
    tile_start = jnp.arange(N_TILES, dtype=jnp.int32) * TILE_M
    n_used = (ends[-1] // TILE_M).astype(jnp.int32)
    tile_e = jnp.sum(tile_start[:, None] >= ends[None, :], axis=1).astype(jnp.int32)
    last_e = tile_e[jnp.maximum(n_used - 1, 0)]
    tile_e = jnp.where(tile_start < ends[-1], tile_e, last_e)
    cstart = jnp.concatenate([jnp.maximum(ends - TILE_M, 0), ends[-1:]])
    cnum = jnp.concatenate([jnp.where(padded > 0, TILE_M // 8, 0), (P_ROWS - ends[-1:]) // 8])
    pad = jnp.concatenate([cstart, cnum]).astype(jnp.int32)
    return offs.astype(jnp.int32), pad, tile_e, n_used.reshape(1)


def _moe_layer(layer, h, meta, cnt, w1, w3, w2):
    offs, pad, tile_e, n_used = _routing_tables(cnt)
    pos = _positions(offs, meta).reshape(N_ASSIGN)
    ys = _expert_ffn(layer, tile_e, n_used, _inverse_positions(pos, pad), h, w1, w3, w2)
    return pos, ys


def _heads_r_major(w):
    return jnp.transpose(w.reshape(N_KV, GQA, HEAD_DIM, -1), (1, 0, 2, 3)).reshape(D, -1)


def _block_mask(rows_per_block, n_rep):
    n = n_rep * N_KV * rows_per_block
    r = (jnp.arange(n)[:, None] // rows_per_block) % N_KV
    c = jnp.arange(KVW)[None, :] // HEAD_DIM
    return (r == c).astype(F32)


def kernel(x_prompt, x_sample, state_conv, cache_k, cache_v, norm_mix, norm_ffn, conv_w_in, conv_b_in, conv_w_dw, conv_b_dw, conv_ln_g, conv_ln_b, conv_w_out, conv_b_out, norm_kv, w_kv, b_kv, attn_w_q, attn_b_q, attn_sinks, attn_w_o, attn_b_o, moe_w_group, moe_b_group, moe_w_router, moe_b_router, moe_w1, moe_w3, moe_w2, norm_final):
    r2 = lambda v: v.reshape(1, -1)
    n_tri = max(TILE_SEQ, TILE_A)
    u_tri = jnp.triu(jnp.ones((n_tri, n_tri), BF16), 1)
    ar0, rb0 = _router_weights(moe_w_group[0], moe_b_group[0], moe_w_router[0], moe_b_router[0])
    ar1, rb1 = _router_weights(moe_w_group[1], moe_b_group[1], moe_w_router[1], moe_b_router[1])

    win = conv_w_in[0].astype(BF16)
    wout = conv_w_out[0].astype(BF16)
    wdw = conv_w_dw[0]
    xs2 = x_sample.reshape(TS, D)
    u_s = _glu_sample(xs2, r2(norm_mix[0]), win, r2(conv_b_in[0]))
    tt = jnp.arange(DEC_SEQ)[:, None]
    jh = jnp.arange(CONV_HIST)[None, :]
    wh = jnp.where((jh >= tt)[..., None], wdw[jnp.clip(jh - tt, 0, CONV_W - 1)], 0.0)
    ju = jnp.arange(DEC_SEQ)[None, :]
    wu = jnp.where((ju <= tt)[..., None], wdw[jnp.clip(CONV_HIST - tt + ju, 0, CONV_W - 1)], 0.0)
    y_s, state_s = _dwconv_sample(u_s.reshape(DEC_BATCH, DEC_SEQ, D), state_conv, wh, wu)
    wdw8 = jnp.broadcast_to(wdw[:, None, :], (CONV_W, 8, D))
    x1, h, meta, cnt, state_p = _conv_mixer(
        x_prompt.reshape(TP, D), xs2, y_s.reshape(TS, D), r2(norm_mix[0]), win, r2(conv_b_in[0]), wdw8,
        r2(conv_b_dw[0]), r2(conv_ln_g[0]), r2(conv_ln_b[0]), wout, r2(conv_b_out[0]), r2(norm_ffn[0]),
        ar0, rb0, u_tri[:TILE_SEQ, :TILE_SEQ])

    pos, ys = _moe_layer(0, h, meta, cnt, moe_w1, moe_w3, moe_w2)
    scale = HEAD_DIM ** -0.5
    wq4 = attn_w_q[0].reshape(D, N_KV, GQA, HEAD_DIM)
    wq = (jnp.transpose(wq4, (0, 2, 1, 3)).reshape(D, D) * scale).astype(BF16)
    bq = r2(_heads_r_major(attn_b_q[0][:, None])[:, 0] * scale)
    wo = _heads_r_major(attn_w_o[0]).astype(BF16)
    x2, kv, q = _combine_kvq(pos, ys, x1, meta, r2(norm_kv), w_kv.astype(BF16), r2(b_kv),
                             r2(norm_mix[1]), wq, bq)

    sinks = attn_sinks[0].astype(F32).reshape(N_KV, GQA).T
    sink_col = jnp.repeat(sinks.reshape(-1), WINDOW)[:, None]
    sink_col8 = jnp.repeat(sinks.reshape(-1), 8)[:, None]
    qi = (jnp.arange(GQA * N_KV * 8) % 8)[:, None]
    vc = ((qi < DEC_SEQ) & (jnp.arange(WINDOW)[None, :] > qi)).astype(F32)
    vn = ((qi < DEC_SEQ) & (jnp.arange(NKN)[None, :] <= qi)).astype(F32)
    q8 = jnp.pad(q[TP:].reshape(DEC_BATCH, DEC_SEQ, D), ((0, 0), (0, 8 - DEC_SEQ), (0, 0)))
    kvn = jnp.pad(kv[TP:].reshape(DEC_BATCH, DEC_SEQ, 2 * KVW), ((0, 0), (0, NKN - DEC_SEQ), (0, 0)))
    o_s8, nk_s, nv_s = _attn_sample(q8, kvn, cache_k.reshape(DEC_BATCH, WINDOW, KVW),
                                    cache_v.reshape(DEC_BATCH, WINDOW, KVW),
                                    _block_mask(8, GQA), vc, vn, sink_col8)
    o_s = o_s8[:, :DEC_SEQ].reshape(TS, D)
    pq = (jnp.arange(N_KV * WINDOW) % WINDOW)[:, None]
    pk = jnp.arange(2 * WINDOW)[None, :]
    dist = pq + WINDOW - pk
    band_rest = (dist >= 0) & (dist < WINDOW)
    band = jnp.stack([band_rest & (pk >= WINDOW), band_rest]).astype(F32)
    x3, h, meta, cnt = _attn_mixer(q, kv, x2, o_s, _block_mask(WINDOW, 1), band, sink_col, wo,
                                   r2(attn_b_o[0]), r2(norm_ffn[1]), ar1, rb1, u_tri[:TILE_A, :TILE_A])

    pos, ys = _moe_layer(1, h, meta, cnt, moe_w1, moe_w3, moe_w2)
    y_p, y_s2 = _combine_final(pos, ys, x3, meta, r2(norm_final))

    kvp = jnp.stack([kv[(b + 1) * SEQ - WINDOW:(b + 1) * SEQ] for b in range(BATCH)])
    new_k_p = kvp[..., :KVW].reshape(BATCH, WINDOW, N_KV, HEAD_DIM)
    new_v_p = kvp[..., KVW:].reshape(BATCH, WINDOW, N_KV, HEAD_DIM)
    return (y_p.reshape(BATCH, SEQ, D), y_s2.reshape(DEC_BATCH, DEC_SEQ, D),
            state_p[None], state_s, new_k_p, new_v_p,
            nk_s.reshape(DEC_BATCH, WINDOW, N_KV, HEAD_DIM), nv_s.reshape(DEC_BATCH, WINDOW, N_KV, HEAD_DIM))
```

```python
import jax
import jax.numpy as jnp
from jax import lax
from jax.experimental import pallas as pl
from jax.experimental.pallas import tpu as pltpu

D = 1024
BATCH = 8
SEQ = 2048
DEC_BATCH = 128
DEC_SEQ = 4
CONV_W = 31
CONV_HIST = CONV_W - 1
HEAD_DIM = 64
N_HEADS = 16
N_KV = 4
GQA = 4
KVW = N_KV * HEAD_DIM
WINDOW = 128
N_GROUPS = 4
EPG = 8
N_EXPERTS = 32
D_EXPERT = 512
RMS_EPS = 1e-5
LN_EPS = 1e-5
MASK_VALUE = -1e30

TP = BATCH * SEQ
TS = DEC_BATCH * DEC_SEQ
T = TP + TS
N_ASSIGN = 2 * T

TILE_SEQ = 512
TILE_M = 512
N_TILES = (N_ASSIGN + N_EXPERTS * (TILE_M - 1)) // TILE_M + 1
P_ROWS = N_TILES * TILE_M
TILE_C = 256
N_LOGIT_ROWS = 48
SAMPLE_BB = 16
NKN = 16
VMEM_LIMIT = 60 * 1024 * 1024

F32 = jnp.float32
BF16 = jnp.bfloat16


def _cparams(n_axes=1):
    return pltpu.CompilerParams(dimension_semantics=("arbitrary",) * n_axes,
                                vmem_limit_bytes=VMEM_LIMIT)


def _const_spec(shape):
    nd = len(shape)
    return pl.BlockSpec(shape, lambda *_: (0,) * nd)


def _rms(x, g):
    return x * lax.rsqrt(jnp.mean(x * x, axis=-1, keepdims=True) + RMS_EPS) * g


def _dot(a, b):
    return jnp.dot(a, b, preferred_element_type=F32)


def _dot_nt(a, b):
    return lax.dot_general(a, b, (((1,), (1,)), ((), ())), preferred_element_type=F32)


def _moe_prologue(x, g_ffn, a_ref, rbias_ref, u_ref, cnt_ref):
    n = x.shape[0]
    h = _rms(x, g_ffn)
    h_hi = h.astype(BF16)
    h_lo = (h - h_hi.astype(F32)).astype(BF16)
    a = a_ref[...]
    l1 = _dot_nt(a, h_hi)
    l2 = _dot_nt(a[:N_LOGIT_ROWS], h_lo)
    logits = l1[:N_LOGIT_ROWS] + l1[N_LOGIT_ROWS:] + l2 + rbias_ref[:, 0:1]
    iota8 = lax.broadcasted_iota(jnp.int32, (8, n), 0).astype(F32)
    gl = logits[0:8]
    gmax = jnp.max(gl, axis=0, keepdims=True)
    g_idx = jnp.min(jnp.where(gl == gmax, iota8, 8.0), axis=0, keepdims=True)
    g_w = 1.0 / jnp.sum(jnp.exp(gl - gmax), axis=0, keepdims=True)
    es = jnp.where(g_idx == 0.0, logits[8:16],
                   jnp.where(g_idx == 1.0, logits[16:24],
                             jnp.where(g_idx == 2.0, logits[24:32], logits[32:40])))
    v1 = jnp.max(es, axis=0, keepdims=True)
    i1 = jnp.min(jnp.where(es == v1, iota8, 8.0), axis=0, keepdims=True)
    es2 = jnp.where(iota8 == i1, -jnp.inf, es)
    v2 = jnp.max(es2, axis=0, keepdims=True)
    i2 = jnp.min(jnp.where(es2 == v2, iota8, 8.0), axis=0, keepdims=True)
    e2x = jnp.exp(v2 - v1)
    w1 = g_w / (1.0 + e2x)
    w2 = g_w * e2x / (1.0 + e2x)
    ex1 = g_idx * EPG + i1
    ex2 = g_idx * EPG + i2
    iota32 = lax.broadcasted_iota(jnp.int32, (N_EXPERTS, n), 0).astype(F32)
    oh1 = jnp.where(iota32 == ex1, 1.0, 0.0)
    oh2 = jnp.where(iota32 == ex2, 1.0, 0.0)
    onehot = oh1 + oh2
    before = _dot(onehot.astype(BF16), u_ref[...]) + cnt_ref[:, 0:1]
    rank1 = jnp.sum(oh1 * before, axis=0, keepdims=True)
    rank2 = jnp.sum(oh2 * before, axis=0, keepdims=True)
    cnt_ref[...] = cnt_ref[...] + jnp.sum(onehot, axis=1, keepdims=True)
    rows = (ex1, ex2, w1, w2, rank1, rank2)
    meta = jnp.zeros((8, n), F32)
    for r, val in enumerate(rows):
        meta = jnp.where(iota8 == r, val, meta)
    return h, meta


def _ln_silu(y, g, b):
    mu = jnp.mean(y, axis=-1, keepdims=True)
    yc = y - mu
    var = jnp.mean(yc * yc, axis=-1, keepdims=True)
    z = yc * lax.rsqrt(var + LN_EPS) * g + b
    return z * jax.nn.sigmoid(z)


CONV_CH = 32
NP_CONV = TP // TILE_SEQ
NSQ = SEQ // TILE_SEQ


def _conv_mixer_body(xp_ref, xs_ref, ys_ref, gmix_ref, win_ref, bin_ref, wdw_ref, bdw_ref,
                     lng_ref, lnb_ref, wout_ref, bout_ref, gffn_ref, a_ref, rbias_ref, u_ref,
                     x1_ref, h_ref, meta_ref, cnt_out_ref, state_ref,
                     ubuf, ush, ybuf, cnt_scr):
    i = pl.program_id(0)
    is_prompt = i < NP_CONV
    s = i % NSQ

    @pl.when(i == 0)
    def _():
        cnt_scr[...] = jnp.zeros_like(cnt_scr)

    @pl.when(is_prompt)
    def _():
        @pl.when(s == 0)
        def _():
            ubuf[0:32, :] = jnp.zeros((32, D), F32)

        hn = _rms(xp_ref[...], gmix_ref[...])
        u2 = _dot(hn.astype(BF16), win_ref[...]) + bin_ref[...]
        ubuf[32:32 + TILE_SEQ, :] = u2[:, :D] * jax.nn.sigmoid(u2[:, D:])

        for sh in range(1, 8):
            ush[sh - 1] = ubuf[sh:sh + TILE_SEQ + 24, :]

        ngrp = CONV_CH // 8

        def chunk(c, carry):
            r0 = pl.multiple_of(c * CONV_CH, CONV_CH)
            for lt in range(D // 128):
                lanes = slice(lt * 128, (lt + 1) * 128)
                acc = [None] * ngrp
                for sh in range(8):
                    a8s = [a8 for a8 in range(5) if 0 <= 8 * a8 + sh - 2 < CONV_W]
                    win = {}
                    for gi in range(a8s[0], a8s[-1] + ngrp):
                        start = pl.multiple_of(r0 + 8 * gi, 8)
                        if sh == 0:
                            win[gi] = ubuf[pl.ds(start, 8), lanes]
                        else:
                            win[gi] = ush[sh - 1, pl.ds(start, 8), lanes]
                    for a8 in a8s:
                        w = wdw_ref[8 * a8 + sh - 2, :, lanes]
                        for gq in range(ngrp):
                            term = w * win[a8 + gq]
                            acc[gq] = term if acc[gq] is None else acc[gq] + term
                for gq in range(ngrp):
                    ybuf[pl.ds(pl.multiple_of(r0 + 8 * gq, 8), 8), lanes] = acc[gq]
            return carry

        lax.fori_loop(0, TILE_SEQ // CONV_CH, chunk, 0)

        @pl.when(s == NSQ - 1)
        def _():
            state_ref[0] = ubuf[TILE_SEQ + 2:TILE_SEQ + 32, :]

        ubuf[0:32, :] = ubuf[TILE_SEQ:TILE_SEQ + 32, :]

    @pl.when(jnp.logical_not(is_prompt))
    def _():
        ybuf[...] = ys_ref[...]

    x = jnp.where(is_prompt, xp_ref[...], xs_ref[...])
    act = _ln_silu(ybuf[...] + bdw_ref[...], lng_ref[...], lnb_ref[...])
    x1 = x + _dot(act.astype(BF16), wout_ref[...]) + bout_ref[...]
    x1_ref[...] = x1
    h, meta = _moe_prologue(x1, gffn_ref[...], a_ref, rbias_ref, u_ref, cnt_scr)
    h_ref[...] = h
    meta_ref[...] = meta
    cnt_out_ref[...] = cnt_scr[...]


def _conv_mixer(xp, xs, ys, gmix, win, b_in, wdw, bdw, lng, lnb, wout, bout, gffn, a, rbias, u):
    consts = [gmix, win, b_in, wdw, bdw, lng, lnb, wout, bout, gffn, a, rbias, u]
    prow = lambda i: (jnp.minimum(i, NP_CONV - 1), 0)
    srow = lambda i: (jnp.maximum(i - NP_CONV, 0), 0)
    row = lambda i: (i, 0)
    return pl.pallas_call(
        _conv_mixer_body,
        grid=(T // TILE_SEQ,),
        in_specs=[pl.BlockSpec((TILE_SEQ, D), prow),
                  pl.BlockSpec((TILE_SEQ, D), srow),
                  pl.BlockSpec((TILE_SEQ, D), srow)] + [_const_spec(c.shape) for c in consts],
        out_specs=[
            pl.BlockSpec((TILE_SEQ, D), row),
            pl.BlockSpec((TILE_SEQ, D), row),
            pl.BlockSpec((8, TILE_SEQ), lambda i: (0, i)),
            _const_spec((N_EXPERTS, 128)),
            pl.BlockSpec((1, CONV_HIST, D), lambda i: (jnp.minimum(i // NSQ, BATCH - 1), 0, 0)),
        ],
        out_shape=[
            jax.ShapeDtypeStruct((T, D), F32),
            jax.ShapeDtypeStruct((T, D), F32),
            jax.ShapeDtypeStruct((8, T), F32),
            jax.ShapeDtypeStruct((N_EXPERTS, 128), F32),
            jax.ShapeDtypeStruct((BATCH, CONV_HIST, D), F32),
        ],
        scratch_shapes=[
            pltpu.VMEM((TILE_SEQ + 32, D), F32),
            pltpu.VMEM((7, TILE_SEQ + 24, D), F32),
            pltpu.VMEM((TILE_SEQ, D), F32),
            pltpu.VMEM((N_EXPERTS, 128), F32),
        ],
        compiler_params=_cparams(),
        name="conv_mixer",
    )(xp, xs, ys, *consts)


def _glu_sample_body(x_ref, gmix_ref, win_ref, bin_ref, u_ref):
    hn = _rms(x_ref[...], gmix_ref[...])
    u2 = _dot(hn.astype(BF16), win_ref[...]) + bin_ref[...]
    u_ref[...] = u2[:, :D] * jax.nn.sigmoid(u2[:, D:])


def _glu_sample(x, gmix, win, b_in):
    args = [x, gmix, win, b_in]
    return pl.pallas_call(
        _glu_sample_body,
        grid=(1,),
        in_specs=[_const_spec(a.shape) for a in args],
        out_specs=_const_spec((TS, D)),
        out_shape=jax.ShapeDtypeStruct((TS, D), F32),
        compiler_params=_cparams(),
        name="glu_sample",
    )(*args)


def _dwconv_sample_body(u_ref, st_ref, wh_ref, wu_ref, y_ref, nst_ref):
    st = st_ref[0]
    u = u_ref[...]
    for t in range(DEC_SEQ):
        y_ref[:, t:t + 1, :] = (jnp.sum(st * wh_ref[t][None], axis=1, keepdims=True)
                                + jnp.sum(u * wu_ref[t][None], axis=1, keepdims=True))
    nst_ref[0, :, 0:CONV_HIST - DEC_SEQ, :] = st_ref[0, :, DEC_SEQ:CONV_HIST, :]
    nst_ref[0, :, CONV_HIST - DEC_SEQ:CONV_HIST, :] = u


def _dwconv_sample(u3, state, wh, wu):
    bb = SAMPLE_BB
    blk = lambda i: (i, 0, 0)
    sblk = lambda i: (0, i, 0, 0)
    return pl.pallas_call(
        _dwconv_sample_body,
        grid=(DEC_BATCH // bb,),
        in_specs=[
            pl.BlockSpec((bb, DEC_SEQ, D), blk),
            pl.BlockSpec((1, bb, CONV_HIST, D), sblk),
            _const_spec(wh.shape), _const_spec(wu.shape),
        ],
        out_specs=[pl.BlockSpec((bb, DEC_SEQ, D), blk), pl.BlockSpec((1, bb, CONV_HIST, D), sblk)],
        out_shape=[
            jax.ShapeDtypeStruct((DEC_BATCH, DEC_SEQ, D), F32),
            jax.ShapeDtypeStruct((1, DEC_BATCH, CONV_HIST, D), F32),
        ],
        compiler_params=_cparams(),
        name="dwconv_sample",
    )(u3, state, wh, wu)


def _positions_body(offs_ref, meta_ref, pos_ref):
    ex = meta_ref[0:2, :]
    acc = meta_ref[4:6, :]
    for e in range(N_EXPERTS):
        acc = acc + jnp.where(ex == float(e), offs_ref[e].astype(F32), 0.0)
    pos_ref[...] = acc.astype(jnp.int32)


def _positions(offs, meta):
    return pl.pallas_call(
        _positions_body,
        grid_spec=pltpu.PrefetchScalarGridSpec(
            num_scalar_prefetch=1,
            grid=(1,),
            in_specs=[pl.BlockSpec((8, T), lambda i, o: (0, 0))],
            out_specs=pl.BlockSpec((2, T), lambda i, o: (0, 0)),
        ),
        out_shape=jax.ShapeDtypeStruct((2, T), jnp.int32),
        compiler_params=_cparams(),
        name="positions",
    )(offs, meta)


def _inverse_body(pos_ref, pad_ref, src_ref):
    def clear(e, carry):
        start = pad_ref[e]

        def chunk(c, cc):
            for j in range(8):
                p = start + c * 8 + j
                src_ref[p] = p & (TP - 1)
            return cc

        return lax.fori_loop(0, pad_ref[N_EXPERTS + 1 + e], chunk, carry)

    lax.fori_loop(0, N_EXPERTS + 1, clear, 0)

    def fill(c, carry):
        for j in range(8):
            t = c * 8 + j
            src_ref[pos_ref[t]] = t
            src_ref[pos_ref[T + t]] = t
        return carry

    lax.fori_loop(0, T // 8, fill, 0)


def _inverse_positions(pos, pad):
    return pl.pallas_call(
        _inverse_body,
        grid_spec=pltpu.PrefetchScalarGridSpec(
            num_scalar_prefetch=2,
            grid=(1,),
            in_specs=[],
            out_specs=pl.BlockSpec(memory_space=pltpu.SMEM),
        ),
        out_shape=jax.ShapeDtypeStruct((P_ROWS,), jnp.int32),
        compiler_params=_cparams(),
        name="inverse_positions",
    )(pos, pad)


def _expert_gather_wait(h_hbm, sem, slot):
    src = h_hbm.at[pl.ds(0, TILE_M), :]
    pltpu.make_async_copy(src, src, sem.at[slot]).wait()


def _expert_body(te_ref, nu_ref, src_ref, h_hbm, w1_ref, w3_ref, w2_ref, y_ref,
                 xbuf_a, xbuf_b, w1b, w3b, w2b, sem):
    i = pl.program_id(0)
    nu = nu_ref[0]
    live = i < nu
    prev = te_ref[jnp.maximum(i - 1, 0)]
    fresh = (i == 0) | (te_ref[i] != prev)
    xbufs = (xbuf_a, xbuf_b)

    @pl.when(i == 0)
    def _():
        def issue(c, carry):
            for j in range(8):
                t = c * 8 + j
                pltpu.make_async_copy(h_hbm.at[pl.ds(src_ref[t], 1), :],
                                      xbuf_a.at[c, pl.ds(j, 1), :], sem.at[0]).start(priority=j % 2)
            return carry

        lax.fori_loop(0, TILE_M // 8, issue, 0)

    @pl.when(live & fresh)
    def _():
        w1b[...] = w1_ref[0, 0].astype(BF16)
        w3b[...] = w3_ref[0, 0].astype(BF16)
        w2b[...] = w2_ref[0, 0].astype(BF16)

    for slot in range(2):
        @pl.when(live & (i % 2 == slot))
        def _(slot=slot):
            _expert_gather_wait(h_hbm, sem, slot)
            base = jnp.minimum(i + 1, nu - 1) * TILE_M
            for t in range(TILE_M):
                pltpu.make_async_copy(h_hbm.at[pl.ds(src_ref[base + t], 1), :],
                                      xbufs[1 - slot].at[t // 8, pl.ds(t % 8, 1), :],
                                      sem.at[1 - slot]).start(priority=t % 2)
            xb = xbufs[slot][...].reshape(TILE_M, D).astype(BF16)
            a = _dot(xb, w1b[...])
            g = _dot(xb, w3b[...])
            act = a * jax.nn.sigmoid(a) * g
            y_ref[...] = _dot(act.astype(BF16), w2b[...])

            @pl.when(i == nu - 1)
            def _():
                _expert_gather_wait(h_hbm, sem, 1 - slot)

    @pl.when(jnp.logical_not(live))
    def _():
        y_ref[...] = jnp.zeros_like(y_ref)


def _expert_ffn(layer, tile_e, n_used, src, h, w1, w3, w2):
    wsel = lambda i, te, nu, sr: (layer, te[i], 0, 0)
    return pl.pallas_call(
        _expert_body,
        grid_spec=pltpu.PrefetchScalarGridSpec(
            num_scalar_prefetch=3,
            grid=(N_TILES,),
            in_specs=[
                pl.BlockSpec(memory_space=pl.ANY),
                pl.BlockSpec((1, 1, D, D_EXPERT), wsel),
                pl.BlockSpec((1, 1, D, D_EXPERT), wsel),
                pl.BlockSpec((1, 1, D_EXPERT, D), wsel),
            ],
            out_specs=pl.BlockSpec((TILE_M, D), lambda i, te, nu, sr: (i, 0)),
            scratch_shapes=[
                pltpu.VMEM((TILE_M // 8, 8, D), F32),
                pltpu.VMEM((TILE_M // 8, 8, D), F32),
                pltpu.VMEM((D, D_EXPERT), BF16),
                pltpu.VMEM((D, D_EXPERT), BF16),
                pltpu.VMEM((D_EXPERT, D), BF16),
                pltpu.SemaphoreType.DMA((2,)),
            ],
        ),
        out_shape=jax.ShapeDtypeStruct((P_ROWS, D), F32),
        compiler_params=_cparams(),
        name="expert_ffn",
    )(tile_e, n_used, src, h, w1, w3, w2)


def _gather_wait(y_hbm, sem, slot):
    src = y_hbm.at[pl.ds(0, TILE_C), :]
    for k in range(2):
        pltpu.make_async_copy(src, src, sem.at[slot]).wait()


def _combine_steps(pos_ref, y_hbm, x_ref, meta_ref, ybufs, sem, finish):
    i = pl.program_id(0)
    n = pl.num_programs(0)

    @pl.when(i == 0)
    def _():
        def issue(c, carry):
            for j in range(8):
                for k in range(2):
                    pltpu.make_async_copy(y_hbm.at[pl.ds(pos_ref[k * T + c * 8 + j], 1), :],
                                          ybufs[0].at[k, c, pl.ds(j, 1), :], sem.at[0]).start(priority=k)
            return carry

        lax.fori_loop(0, TILE_C // 8, issue, 0)

    for slot in range(2):
        @pl.when(i % 2 == slot)
        def _(slot=slot):
            _gather_wait(y_hbm, sem, slot)
            base = jnp.minimum(i + 1, n - 1) * TILE_C
            for t in range(TILE_C):
                for k in range(2):
                    pltpu.make_async_copy(y_hbm.at[pl.ds(pos_ref[k * T + base + t], 1), :],
                                          ybufs[1 - slot].at[k, t // 8, pl.ds(t % 8, 1), :],
                                          sem.at[1 - slot]).start(priority=k)
            mt = jnp.concatenate([meta_ref[...], jnp.zeros((120, TILE_C), F32)], axis=0).T
            y0 = ybufs[slot][0].reshape(TILE_C, D)
            y1 = ybufs[slot][1].reshape(TILE_C, D)
            finish(x_ref[...] + mt[:, 2:3] * y0 + mt[:, 3:4] * y1)

            @pl.when(i == n - 1)
            def _():
                _gather_wait(y_hbm, sem, 1 - slot)


def _combine_scratch():
    buf = pltpu.VMEM((2, TILE_C // 8, 8, D), F32)
    return [buf, buf, pltpu.SemaphoreType.DMA((2,))]


def _combine_kvq_body(pos_ref, y_hbm, x_ref, meta_ref, gkv_ref, wkv_ref, bkv_ref, gq_ref, wq_ref, bq_ref,
                      x2_ref, kv_ref, q_ref, ybuf_a, ybuf_b, sem):
    def finish(x2):
        x2_ref[...] = x2
        kv_ref[...] = _dot(_rms(x2, gkv_ref[...]).astype(BF16), wkv_ref[...]) + bkv_ref[...]
        q_ref[...] = _dot(_rms(x2, gq_ref[...]).astype(BF16), wq_ref[...]) + bq_ref[...]

    _combine_steps(pos_ref, y_hbm, x_ref, meta_ref, (ybuf_a, ybuf_b), sem, finish)


def _combine_kvq(pos, ys, x1, meta, gkv, wkv, bkv, gq, wq, bq):
    row = lambda i, p: (i, 0)
    consts = [gkv, wkv, bkv, gq, wq, bq]
    return pl.pallas_call(
        _combine_kvq_body,
        grid_spec=pltpu.PrefetchScalarGridSpec(
            num_scalar_prefetch=1,
            grid=(T // TILE_C,),
            in_specs=[pl.BlockSpec(memory_space=pl.ANY),
                      pl.BlockSpec((TILE_C, D), row),
                      pl.BlockSpec((8, TILE_C), lambda i, p: (0, i))]
                     + [pl.BlockSpec(c.shape, lambda i, p, nd=c.ndim: (0,) * nd) for c in consts],
            out_specs=[pl.BlockSpec((TILE_C, D), row),
                       pl.BlockSpec((TILE_C, 2 * KVW), row),
                       pl.BlockSpec((TILE_C, D), row)],
            scratch_shapes=_combine_scratch(),
        ),
        out_shape=[jax.ShapeDtypeStruct((T, D), F32),
                   jax.ShapeDtypeStruct((T, 2 * KVW), F32),
                   jax.ShapeDtypeStruct((T, D), F32)],
        compiler_params=_cparams(),
        name="combine_kvq",
    )(pos, ys, x1, meta, *consts)


def _combine_final_body(pos_ref, y_hbm, x_ref, meta_ref, gf_ref, yp_ref, ys_ref, ybuf_a, ybuf_b, sem):
    i = pl.program_id(0)

    def finish(x4):
        out = _rms(x4, gf_ref[...])

        @pl.when(i < TP // TILE_C)
        def _():
            yp_ref[...] = out

        @pl.when(i >= TP // TILE_C)
        def _():
            ys_ref[...] = out

    _combine_steps(pos_ref, y_hbm, x_ref, meta_ref, (ybuf_a, ybuf_b), sem, finish)


def _combine_final(pos, ys, x3, meta, gf):
    row = lambda i, p: (i, 0)
    npt = TP // TILE_C
    return pl.pallas_call(
        _combine_final_body,
        grid_spec=pltpu.PrefetchScalarGridSpec(
            num_scalar_prefetch=1,
            grid=(T // TILE_C,),
            in_specs=[pl.BlockSpec(memory_space=pl.ANY),
                      pl.BlockSpec((TILE_C, D), row),
                      pl.BlockSpec((8, TILE_C), lambda i, p: (0, i)),
                      pl.BlockSpec((1, D), lambda i, p: (0, 0))],
            out_specs=[pl.BlockSpec((TILE_C, D), lambda i, p: (jnp.minimum(i, npt - 1), 0)),
                       pl.BlockSpec((TILE_C, D), lambda i, p: (jnp.maximum(i - npt, 0), 0))],
            scratch_shapes=_combine_scratch(),
        ),
        out_shape=[jax.ShapeDtypeStruct((TP, D), F32), jax.ShapeDtypeStruct((TS, D), F32)],
        compiler_params=_cparams(),
        name="combine_final",
    )(pos, ys, x3, meta, gf)


ATT_SUB = 4
ATT_GROUP = 8
TILE_A = ATT_SUB * WINDOW
NB_ATT = SEQ // TILE_A
NP_ATT = TP // TILE_A


def _attn_mixer_body(q_ref, kvc_ref, kvp_ref, x_ref, os_ref, bm_ref, band_ref, sink_ref,
                     wo_ref, bo_ref, gffn_ref, a_ref, rbias_ref, u_ref,
                     x3_ref, h_ref, meta_ref, cnt_out_ref, obuf, cnt_scr):
    i = pl.program_id(0)
    is_prompt = i < NP_ATT
    j = i % NB_ATT

    @pl.when(i == 0)
    def _():
        cnt_scr[...] = jnp.zeros_like(cnt_scr)

    @pl.when(is_prompt)
    def _():
        kall = jnp.concatenate([kvp_ref[:, :KVW], kvc_ref[:, :KVW]], axis=0).astype(BF16)
        vall = jnp.concatenate([kvp_ref[:, KVW:], kvc_ref[:, KVW:]], axis=0).astype(BF16)
        bm = bm_ref[...]
        bm16 = bm.astype(BF16)
        kbs = [kall[sb * WINDOW:(sb + 2) * WINDOW] for sb in range(ATT_SUB)]
        vbs = [vall[sb * WINDOW:(sb + 2) * WINDOW] for sb in range(ATT_SUB)]
        qbs = [q_ref[sb * WINDOW:(sb + 1) * WINDOW, :].astype(BF16) for sb in range(ATT_SUB)]
        valids = [band_ref[jnp.minimum(j, 1) if sb == 0 else 1] > 0.0 for sb in range(ATT_SUB)]
        all_units = [(sb, r) for sb in range(ATT_SUB) for r in range(GQA)]
        for u0 in range(0, len(all_units), ATT_GROUP):
            units = all_units[u0:u0 + ATT_GROUP]
            scores = []
            for sb, r in units:
                qm = jnp.concatenate([qbs[sb][:, r * KVW:(r + 1) * KVW]] * N_KV, axis=0) * bm16
                scores.append(_dot_nt(qm, kbs[sb]))
            probs, scales = [], []
            for (sb, r), s in zip(units, scores):
                s = jnp.where(valids[sb], s, MASK_VALUE)
                sink = sink_ref[r * N_KV * WINDOW:(r + 1) * N_KV * WINDOW]
                m = jnp.maximum(jnp.max(s, axis=-1, keepdims=True), sink)
                p = jnp.exp(s - m)
                den = jnp.sum(p, axis=-1, keepdims=True) + jnp.exp(sink - m)
                probs.append(p.astype(BF16))
                scales.append(1.0 / den)
            outs = [_dot(p, vbs[sb]) for (sb, r), p in zip(units, probs)]
            for (sb, r), o, sc in zip(units, outs, scales):
                o = o * (bm * sc)
                acc = o[0:WINDOW]
                for g in range(1, N_KV):
                    acc = acc + o[g * WINDOW:(g + 1) * WINDOW]
                obuf[sb * WINDOW:(sb + 1) * WINDOW, r * KVW:(r + 1) * KVW] = acc

    @pl.when(jnp.logical_not(is_prompt))
    def _():
        obuf[...] = os_ref[...]

    x3 = x_ref[...] + _dot(obuf[...].astype(BF16), wo_ref[...]) + bo_ref[...]
    x3_ref[...] = x3
    h, meta = _moe_prologue(x3, gffn_ref[...], a_ref, rbias_ref, u_ref, cnt_scr)
    h_ref[...] = h
    meta_ref[...] = meta
    cnt_out_ref[...] = cnt_scr[...]


def _attn_mixer(q, kv, x2, o_s, bm, band, sink_col, wo, bo, gffn, a, rbias, u):
    consts = [bm, band, sink_col, wo, bo, gffn, a, rbias, u]
    prow = lambda i: (jnp.minimum(i, NP_ATT - 1), 0)

    def prev(i):
        ic = jnp.minimum(i, NP_ATT - 1)
        return (ATT_SUB * ic - jnp.where(ic % NB_ATT == 0, 0, 1), 0)

    row = lambda i: (i, 0)
    return pl.pallas_call(
        _attn_mixer_body,
        grid=(T // TILE_A,),
        in_specs=[pl.BlockSpec((TILE_A, D), prow),
                  pl.BlockSpec((TILE_A, 2 * KVW), prow),
                  pl.BlockSpec((WINDOW, 2 * KVW), prev),
                  pl.BlockSpec((TILE_A, D), row),
                  pl.BlockSpec((TILE_A, D), lambda i: (jnp.maximum(i - NP_ATT, 0), 0))]
                 + [_const_spec(c.shape) for c in consts],
        out_specs=[pl.BlockSpec((TILE_A, D), row),
                   pl.BlockSpec((TILE_A, D), row),
                   pl.BlockSpec((8, TILE_A), lambda i: (0, i)),
                   _const_spec((N_EXPERTS, 128))],
        out_shape=[jax.ShapeDtypeStruct((T, D), F32),
                   jax.ShapeDtypeStruct((T, D), F32),
                   jax.ShapeDtypeStruct((8, T), F32),
                   jax.ShapeDtypeStruct((N_EXPERTS, 128), F32)],
        scratch_shapes=[pltpu.VMEM((TILE_A, D), F32), pltpu.VMEM((N_EXPERTS, 128), F32)],
        compiler_params=_cparams(),
        name="attn_mixer",
    )(q, kv, kv, x2, o_s, *consts)


def _attn_sample_body(q_ref, kvn_ref, ck_ref, cv_ref, bm_ref, vc_ref, vn_ref, sink_ref,
                      o_ref, nk_ref, nv_ref):
    bm = bm_ref[...]
    valid_c = vc_ref[...] > 0.0
    valid_n = vn_ref[...] > 0.0
    sink = sink_ref[...]
    hist = WINDOW - DEC_SEQ

    def one(b, carry):
        q8 = q_ref[b]
        kvn = kvn_ref[b]
        qm = jnp.concatenate(
            [q8[:, r * KVW:(r + 1) * KVW] for r in range(GQA) for _ in range(N_KV)], axis=0)
        qm = (qm * bm).astype(BF16)
        s_c = jnp.where(valid_c, _dot_nt(qm, ck_ref[b].astype(BF16)), MASK_VALUE)
        s_n = jnp.where(valid_n, _dot_nt(qm, kvn[:, :KVW].astype(BF16)), MASK_VALUE)
        m = jnp.maximum(jnp.maximum(jnp.max(s_c, axis=-1, keepdims=True),
                                    jnp.max(s_n, axis=-1, keepdims=True)), sink)
        p_c = jnp.exp(s_c - m)
        p_n = jnp.exp(s_n - m)
        den = (jnp.sum(p_c, axis=-1, keepdims=True) + jnp.sum(p_n, axis=-1, keepdims=True)
               + jnp.exp(sink - m))
        o = (_dot(p_c.astype(BF16), cv_ref[b].astype(BF16))
             + _dot(p_n.astype(BF16), kvn[:, KVW:].astype(BF16))) * (bm * (1.0 / den))
        outs = []
        for r in range(GQA):
            acc = o[r * 32:r * 32 + 8]
            for g in range(1, N_KV):
                acc = acc + o[r * 32 + g * 8:r * 32 + g * 8 + 8]
            outs.append(acc)
        o_ref[b] = jnp.concatenate(outs, axis=1)
        nk_ref[b, 0:hist, :] = ck_ref[b, DEC_SEQ:WINDOW, :]
        nk_ref[b, hist:WINDOW, :] = kvn_ref[b, 0:DEC_SEQ, 0:KVW]
        nv_ref[b, 0:hist, :] = cv_ref[b, DEC_SEQ:WINDOW, :]
        nv_ref[b, hist:WINDOW, :] = kvn_ref[b, 0:DEC_SEQ, KVW:2 * KVW]
        return carry

    lax.fori_loop(0, SAMPLE_BB, one, 0, unroll=2)


def _attn_sample(q8, kvn16, ck, cv, bm, vc, vn, sink_col8):
    bb = SAMPLE_BB
    blk = lambda i: (i, 0, 0)
    consts = [bm, vc, vn, sink_col8]
    return pl.pallas_call(
        _attn_sample_body,
        grid=(DEC_BATCH // bb,),
        in_specs=[pl.BlockSpec((bb, 8, D), blk),
                  pl.BlockSpec((bb, NKN, 2 * KVW), blk),
                  pl.BlockSpec((bb, WINDOW, KVW), blk),
                  pl.BlockSpec((bb, WINDOW, KVW), blk)] + [_const_spec(c.shape) for c in consts],
        out_specs=[pl.BlockSpec((bb, 8, D), blk),
                   pl.BlockSpec((bb, WINDOW, KVW), blk),
                   pl.BlockSpec((bb, WINDOW, KVW), blk)],
        out_shape=[jax.ShapeDtypeStruct((DEC_BATCH, 8, D), F32),
                   jax.ShapeDtypeStruct((DEC_BATCH, WINDOW, KVW), F32),
                   jax.ShapeDtypeStruct((DEC_BATCH, WINDOW, KVW), F32)],
        compiler_params=_cparams(),
        name="attn_sample",
    )(q8, kvn16, ck, cv, *consts)


def _router_weights(w_group, b_group, w_router, b_router):
    wt = jnp.zeros((N_LOGIT_ROWS, D), F32)
    wt = wt.at[0:N_GROUPS].set(w_group.T)
    wt = wt.at[8:8 + N_EXPERTS].set(jnp.transpose(w_router, (0, 2, 1)).reshape(N_EXPERTS, D))
    hi = wt.astype(BF16)
    lo = (wt - hi.astype(F32)).astype(BF16)
    a = jnp.concatenate([hi, lo], axis=0)
    bias = jnp.zeros((N_LOGIT_ROWS,), F32)
    bias = bias.at[0:N_GROUPS].set(b_group)
    bias = bias.at[N_GROUPS:8].set(MASK_VALUE)
    bias = bias.at[8:8 + N_EXPERTS].set(b_router.reshape(N_EXPERTS))
    return a, jnp.broadcast_to(bias[:, None], (N_LOGIT_ROWS, 128))


def _routing_tables(cnt):
    counts = cnt[:, 0].astype(jnp.int32)
    padded = ((counts + TILE_M - 1) // TILE_M) * TILE_M
    ends = jnp.cumsum(padded)
    offs = ends - padded
    tile_start = jnp.arange(N_TILES, dtype=jnp.int32) * TILE_M
    n_used = (ends[-1] // TILE_M).astype(jnp.int32)
    tile_e = jnp.sum(tile_start[:, None] >= ends[None, :], axis=1).astype(jnp.int32)
    last_e = tile_e[jnp.maximum(n_used - 1, 0)]
    tile_e = jnp.where(tile_start < ends[-1], tile_e, last_e)
    cstart = jnp.concatenate([(offs + counts) // 8 * 8, ends[-1:]])
    cend = jnp.concatenate([ends, jnp.full((1,), P_ROWS, jnp.int32)])
    pad = jnp.concatenate([cstart, (cend - cstart) // 8]).astype(jnp.int32)
    return offs.astype(jnp.int32), pad, tile_e, n_used.reshape(1)


def _moe_layer(layer, h, meta, cnt, w1, w3, w2):
    offs, pad, tile_e, n_used = _routing_tables(cnt)
    pos = _positions(offs, meta).reshape(N_ASSIGN)
    ys = _expert_ffn(layer, tile_e, n_used, _inverse_positions(pos, pad), h, w1, w3, w2)
    return pos, ys


def _heads_r_major(w):
    return jnp.transpose(w.reshape(N_KV, GQA, HEAD_DIM, -1), (1, 0, 2, 3)).reshape(D, -1)


def _block_mask(rows_per_block, n_rep):
    n = n_rep * N_KV * rows_per_block
    r = (jnp.arange(n)[:, None] // rows_per_block) % N_KV
    c = jnp.arange(KVW)[None, :] // HEAD_DIM
    return (r == c).astype(F32)


def kernel(x_prompt, x_sample, state_conv, cache_k, cache_v, norm_mix, norm_ffn, conv_w_in, conv_b_in, conv_w_dw, conv_b_dw, conv_ln_g, conv_ln_b, conv_w_out, conv_b_out, norm_kv, w_kv, b_kv, attn_w_q, attn_b_q, attn_sinks, attn_w_o, attn_b_o, moe_w_group, moe_b_group, moe_w_router, moe_b_router, moe_w1, moe_w3, moe_w2, norm_final):
    r2 = lambda v: v.reshape(1, -1)
    n_tri = max(TILE_SEQ, TILE_A)
    u_tri = jnp.triu(jnp.ones((n_tri, n_tri), BF16), 1)
    ar0, rb0 = _router_weights(moe_w_group[0], moe_b_group[0], moe_w_router[0], moe_b_router[0])
    ar1, rb1 = _router_weights(moe_w_group[1], moe_b_group[1], moe_w_router[1], moe_b_router[1])

    win = conv_w_in[0].astype(BF16)
    wout = conv_w_out[0].astype(BF16)
    wdw = conv_w_dw[0]
    xs2 = x_sample.reshape(TS, D)
    u_s = _glu_sample(xs2, r2(norm_mix[0]), win, r2(conv_b_in[0]))
    tt = jnp.arange(DEC_SEQ)[:, None]
    jh = jnp.arange(CONV_HIST)[None, :]
    wh = jnp.where((jh >= tt)[..., None], wdw[jnp.clip(jh - tt, 0, CONV_W - 1)], 0.0)
    ju = jnp.arange(DEC_SEQ)[None, :]
    wu = jnp.where((ju <= tt)[..., None], wdw[jnp.clip(CONV_HIST - tt + ju, 0, CONV_W - 1)], 0.0)
    y_s, state_s = _dwconv_sample(u_s.reshape(DEC_BATCH, DEC_SEQ, D), state_conv, wh, wu)
    wdw8 = jnp.broadcast_to(wdw[:, None, :], (CONV_W, 8, D))
    x1, h, meta, cnt, state_p = _conv_mixer(
        x_prompt.reshape(TP, D), xs2, y_s.reshape(TS, D), r2(norm_mix[0]), win, r2(conv_b_in[0]), wdw8,
        r2(conv_b_dw[0]), r2(conv_ln_g[0]), r2(conv_ln_b[0]), wout, r2(conv_b_out[0]), r2(norm_ffn[0]),
        ar0, rb0, u_tri[:TILE_SEQ, :TILE_SEQ])

    pos, ys = _moe_layer(0, h, meta, cnt, moe_w1, moe_w3, moe_w2)
    scale = HEAD_DIM ** -0.5
    wq4 = attn_w_q[0].reshape(D, N_KV, GQA, HEAD_DIM)
    wq = (jnp.transpose(wq4, (0, 2, 1, 3)).reshape(D, D) * scale).astype(BF16)
    bq = r2(_heads_r_major(attn_b_q[0][:, None])[:, 0] * scale)
    wo = _heads_r_major(attn_w_o[0]).astype(BF16)
    x2, kv, q = _combine_kvq(pos, ys, x1, meta, r2(norm_kv), w_kv.astype(BF16), r2(b_kv),
                             r2(norm_mix[1]), wq, bq)

    sinks = attn_sinks[0].astype(F32).reshape(N_KV, GQA).T
    sink_col = jnp.repeat(sinks.reshape(-1), WINDOW)[:, None]
    sink_col8 = jnp.repeat(sinks.reshape(-1), 8)[:, None]
    qi = (jnp.arange(GQA * N_KV * 8) % 8)[:, None]
    vc = ((qi < DEC_SEQ) & (jnp.arange(WINDOW)[None, :] > qi)).astype(F32)
    vn = ((qi < DEC_SEQ) & (jnp.arange(NKN)[None, :] <= qi)).astype(F32)
    q8 = jnp.pad(q[TP:].reshape(DEC_BATCH, DEC_SEQ, D), ((0, 0), (0, 8 - DEC_SEQ), (0, 0)))
    kvn = jnp.pad(kv[TP:].reshape(DEC_BATCH, DEC_SEQ, 2 * KVW), ((0, 0), (0, NKN - DEC_SEQ), (0, 0)))
    o_s8, nk_s, nv_s = _attn_sample(q8, kvn, cache_k.reshape(DEC_BATCH, WINDOW, KVW),
                                    cache_v.reshape(DEC_BATCH, WINDOW, KVW),
                                    _block_mask(8, GQA), vc, vn, sink_col8)
    o_s = o_s8[:, :DEC_SEQ].reshape(TS, D)
    pq = (jnp.arange(N_KV * WINDOW) % WINDOW)[:, None]
    pk = jnp.arange(2 * WINDOW)[None, :]
    dist = pq + WINDOW - pk
    band_rest = (dist >= 0) & (dist < WINDOW)
    band = jnp.stack([band_rest & (pk >= WINDOW), band_rest]).astype(F32)
    x3, h, meta, cnt = _attn_mixer(q, kv, x2, o_s, _block_mask(WINDOW, 1), band, sink_col, wo,
                                   r2(attn_b_o[0]), r2(norm_ffn[1]), ar1, rb1, u_tri[:TILE_A, :TILE_A])

    pos, ys = _moe_layer(1, h, meta, cnt, moe_w1, moe_w3, moe_w2)
    y_p, y_s2 = _combine_final(pos, ys, x3, meta, r2(norm_final))

    kvp = jnp.stack([kv[(b + 1) * SEQ - WINDOW:(b + 1) * SEQ] for b in range(BATCH)])
    new_k_p = kvp[..., :KVW].reshape(BATCH, WINDOW, N_KV, HEAD_DIM)
    new_v_p = kvp[..., KVW:].reshape(BATCH, WINDOW, N_KV, HEAD_DIM)
    return (y_p.reshape(BATCH, SEQ, D), y_s2.reshape(DEC_BATCH, DEC_SEQ, D),
            state_p[None], state_s, new_k_p, new_v_p,
            nk_s.reshape(DEC_BATCH, WINDOW, N_KV, HEAD_DIM), nv_s.reshape(DEC_BATCH, WINDOW, N_KV, HEAD_DIM))
```
